```python
import math
import jax
import jax.numpy as jnp
from jax import lax
import numpy as np

D_MODEL = 1024
BATCH = 16
SEQ = 2048
DEPTH = 4

GRID_W = 64
CTX_LEN = 256
N_BRANCH = 4
BR_W = 512
A_HEADS = 4
A_DQK = 64
A_DV = 2 * A_DQK
A_QBLOCK = 128
ROPE_BASE = 10000.0
HY_W = BR_W
HY_ORDER = 2
HY_SHORT = 3
HY_BANDS = 16
HY_EMB = 1 + 2 * HY_BANDS
HY_HID = 64
HY_MIN_DECAY = math.log(1e-2) / 1.5
HY_MAX_DECAY = math.log(1e-2) / 0.3
FN_W = BR_W
FN_GROUPS = 4
DN_HEADS = 4
DN_DK = 128
DN_DV = 128
DN_SHORT = 3
DN_CHUNK = 64
D_FF = 4 * D_MODEL
EPS = 1e-6

OFF_AQ = 0
OFF_AK = OFF_AQ + A_HEADS * 2 * A_DQK
OFF_AV = OFF_AK + A_HEADS * 2 * A_DQK
OFF_HY = OFF_AV + A_HEADS * A_DV
OFF_FN = OFF_HY + (HY_ORDER + 1) * HY_W
OFF_DN = OFF_FN + FN_W
OFF_DZ = OFF_DN + DN_HEADS * (2 * DN_DK + DN_DV)
OFF_DA = OFF_DZ + DN_HEADS * DN_DV
OFF_DB = OFF_DA + 2 * DN_HEADS
N_IN = OFF_DB + 2 * DN_HEADS

kernel_name = 'hybrid_diffattn_hyena_fnet_gdn_dit'

F32 = jnp.float32


def rmsnorm(x, w):
    xf = x.astype(F32)
    y = xf * lax.rsqrt(jnp.mean(xf * xf, axis=-1, keepdims=True) + EPS)
    return (y * w.astype(F32)).astype(x.dtype)


def l2norm(x):
    return x * lax.rsqrt(jnp.sum(x * x, axis=-1, keepdims=True) + EPS)


def centred_conv(u, w):
    k = w.shape[0]
    r = k // 2
    length = u.shape[1]
    up = jnp.pad(u, ((0, 0), (r, r), (0, 0)))
    out = up[:, 0:length] * w[0]
    for i in range(1, k):
        out = out + up[:, i:i + length] * w[i]
    return out


def axial_rope_tables(length):
    rows = length // GRID_W
    row = jnp.repeat(jnp.arange(rows, dtype=F32), GRID_W)
    col = jnp.tile(jnp.arange(GRID_W, dtype=F32), rows)
    n_freq = A_DQK // 4
    inv = ROPE_BASE ** (-jnp.arange(n_freq, dtype=F32) / n_freq)
    ang = jnp.stack([row[:, None] * inv, col[:, None] * inv], axis=1)
    return jnp.cos(ang), jnp.sin(ang)


def apply_axial_rope(x, cos, sin):
    sh = x.shape
    xr = x.astype(F32).reshape(sh[:-1] + (2, 2, A_DQK // 4))
    xa, xb = xr[..., 0, :], xr[..., 1, :]
    cs, sn = cos[:, None, None], sin[:, None, None]
    out = jnp.stack([xa * cs - xb * sn, xb * cs + xa * sn], axis=-2)
    return out.reshape(sh).astype(x.dtype)


def diff_softmax(q, k, v, lam):
    s = jnp.einsum('bhqcd,bhkcd->bhcqk', q.astype(F32), k.astype(F32)) * (A_DQK ** -0.5)
    p = jax.nn.softmax(s, axis=-1)
    a = p[:, :, 0] - lam * p[:, :, 1]
    return jnp.einsum('bhqk,bhkv->bhqv', a, v.astype(F32))


def diff_attention_branch(p, pc, qk_gain, lam_vec, subln_w, lambda_init, need_ctx):
    def split_heads(pp):
        b_, l_, _ = pp.shape
        q = pp[..., OFF_AQ:OFF_AK].reshape(b_, l_, A_HEADS, 2, A_DQK)
        k = pp[..., OFF_AK:OFF_AV].reshape(b_, l_, A_HEADS, 2, A_DQK)
        v = pp[..., OFF_AV:OFF_HY].reshape(b_, l_, A_HEADS, A_DV)
        return rmsnorm(q, qk_gain[0]), rmsnorm(k, qk_gain[1]), v

    b, l = p.shape[:2]
    q, k, v = split_heads(p)
    qc, kc, vc = split_heads(pc)
    cos, sin = axial_rope_tables(l)
    q = apply_axial_rope(q, cos, sin)
    k = apply_axial_rope(k, cos, sin)
    lf = lam_vec.astype(F32)
    lam = jnp.exp(jnp.sum(lf[0] * lf[1])) - jnp.exp(jnp.sum(lf[2] * lf[3])) + lambda_init
    kc_t = kc.transpose(0, 2, 1, 3, 4)
    vc_t = vc.transpose(0, 2, 1, 3)
    k_all = jnp.concatenate([kc_t, k.transpose(0, 2, 1, 3, 4)], axis=2)
    v_all = jnp.concatenate([vc_t, v.transpose(0, 2, 1, 3)], axis=2)
    n_blk = l // A_QBLOCK
    qb = q.reshape(b, n_blk, A_QBLOCK, A_HEADS, 2, A_DQK).transpose(1, 0, 3, 2, 4, 5)
    o = lax.map(lambda blk: diff_softmax(blk, k_all, v_all, lam), qb)
    o = o.transpose(1, 0, 3, 2, 4).reshape(b, l, A_HEADS, A_DV)

    def finish(oo):
        b_, l_ = oo.shape[:2]
        y = rmsnorm(oo, subln_w) * (1.0 - lambda_init)
        return y.reshape(b_, l_, A_HEADS * A_DV).astype(p.dtype)

    y = finish(o)
    yc = None
    if need_ctx:
        oc = diff_softmax(qc.transpose(0, 2, 1, 3, 4), kc_t, vc_t, lam).transpose(0, 2, 1, 3)
        yc = finish(oc)
    return y, yc


def hyena_filter_spectra(length, w1, b1, freq, w2, b2, w3):
    t = jnp.linspace(0.0, 1.0, length, dtype=F32)[:, None]
    w = 2.0 * math.pi * jnp.arange(length, dtype=F32)[:, None] / length
    bands = jnp.linspace(1e-4, HY_BANDS - 1, HY_BANDS, dtype=F32)[None]
    feats = jnp.concatenate([t, jnp.cos(bands * w), -jnp.sin(bands * w)], axis=-1)
    hdn = jnp.sin(freq[0].astype(F32) * (feats @ w1.astype(F32) + b1.astype(F32)))
    hdn = jnp.sin(freq[1].astype(F32) * (hdn @ w2.astype(F32) + b2.astype(F32)))
    h = (hdn @ w3.astype(F32)).reshape(length, HY_ORDER, 2, HY_W)
    deltas = jnp.abs(jnp.linspace(HY_MIN_DECAY, HY_MAX_DECAY, HY_W, dtype=F32))
    h = h * jnp.exp(-t[:, :, None, None] * deltas)
    taps = jnp.concatenate([h[:, :, 0], jnp.zeros((1, HY_ORDER, HY_W), F32), h[:0:-1, :, 1]], axis=0)
    taps = taps / jnp.sum(jnp.abs(taps), axis=0, keepdims=True)
    return jnp.fft.rfft(taps, n=2 * length, axis=0)


def long_conv(z, kf, bias):
    length = z.shape[1]
    zf = jnp.fft.rfft(z, n=2 * length, axis=1)
    return jnp.fft.irfft(zf * kf, n=2 * length, axis=1)[:, :length] + bias * z


def hyena_branch(pp, conv_w, w1, b1, freq, w2, b2, w3, bias):
    length = pp.shape[1]
    u = centred_conv(pp[..., OFF_HY:OFF_FN], conv_w).astype(F32)
    parts = jnp.split(u, HY_ORDER + 1, axis=-1)
    kf = hyena_filter_spectra(length, w1, b1, freq, w2, b2, w3)
    z = parts[-1]
    for n in range(HY_ORDER):
        z = parts[n] * long_conv(z, kf[:, n], bias[n].astype(F32))
    return z.astype(pp.dtype)


def fourier_branch(pp):
    b, l, _ = pp.shape
    u = pp[..., OFF_FN:OFF_DN].astype(F32).reshape(b, l, FN_GROUPS, FN_W // FN_GROUPS)
    y = jnp.fft.fft2(u, axes=(1, 3), norm='ortho').real
    return y.reshape(b, l, FN_W).astype(pp.dtype)


def gated_delta_chunked(q, k, v, g, beta, s0):
    b, h, l, dk = q.shape
    dv = v.shape[-1]
    cs = DN_CHUNK
    n = l // cs
    q = (q * dk ** -0.5).reshape(b, h, n, cs, dk)
    k = k.reshape(b, h, n, cs, dk)
    v = v.reshape(b, h, n, cs, dv)
    beta = beta.reshape(b, h, n, cs)
    gc = jnp.cumsum(g.reshape(b, h, n, cs), axis=-1)
    idx = jnp.arange(cs)
    incl = idx[:, None] >= idx[None, :]
    strict = idx[:, None] > idx[None, :]
    diff = gc[..., :, None] - gc[..., None, :]
    decay = jnp.where(incl, jnp.exp(jnp.where(incl, diff, 0.0)), 0.0)
    kb = k * beta[..., None]
    a_mat = jnp.where(strict, jnp.einsum('bhncd,bhnsd->bhncs', kb, k) * decay, 0.0)
    eye = jnp.eye(cs, dtype=q.dtype)
    t_mat = lax.linalg.triangular_solve(a_mat + eye, jnp.broadcast_to(eye, a_mat.shape),
                                        left_side=True, lower=True, unit_diagonal=True)
    u_c = t_mat @ (v * beta[..., None])
    w_c = t_mat @ (kb * jnp.exp(gc)[..., None])
    intra = jnp.where(incl, jnp.einsum('bhncd,bhnsd->bhncs', q, k) * decay, 0.0)
    q_dec = q * jnp.exp(gc)[..., None]
    k_dec = k * jnp.exp(gc[..., -1:] - gc)[..., None]
    g_last = jnp.exp(gc[..., -1])
    xs = (jnp.moveaxis(u_c, 2, 0), jnp.moveaxis(w_c, 2, 0), jnp.moveaxis(intra, 2, 0),
          jnp.moveaxis(q_dec, 2, 0), jnp.moveaxis(k_dec, 2, 0), jnp.moveaxis(g_last, 2, 0))

    def step(s, inp):
        u_i, w_i, a_i, qd_i, kd_i, gl_i = inp
        v_new = u_i - jnp.einsum('bhck,bhkv->bhcv', w_i, s)
        o_i = jnp.einsum('bhck,bhkv->bhcv', qd_i, s) + jnp.einsum('bhcs,bhsv->bhcv', a_i, v_new)
        s = s * gl_i[..., None, None] + jnp.einsum('bhck,bhcv->bhkv', kd_i, v_new)
        return s, o_i

    s_fin, o = lax.scan(step, s0, xs)
    o = jnp.moveaxis(o, 0, 2).reshape(b, h, l, dv)
    return o, s_fin


def _flip(t, rev):
    return jnp.flip(t, axis=2) if rev else t


def deltanet_branch(p, pc, conv_w, a_log, dt_bias, norm_w, need_ctx):
    def prep(pp):
        b_, l_, _ = pp.shape
        qkv = jax.nn.silu(centred_conv(pp[..., OFF_DN:OFF_DZ], conv_w).astype(F32))
        q, k, v = jnp.split(qkv, [DN_HEADS * DN_DK, 2 * DN_HEADS * DN_DK], axis=-1)
        q = l2norm(q.reshape(b_, l_, DN_HEADS, DN_DK)).transpose(0, 2, 1, 3)
        k = l2norm(k.reshape(b_, l_, DN_HEADS, DN_DK)).transpose(0, 2, 1, 3)
        v = v.reshape(b_, l_, DN_HEADS, DN_DV).transpose(0, 2, 1, 3)
        a = pp[..., OFF_DA:OFF_DB].astype(F32).reshape(b_, l_, 2, DN_HEADS)
        bt = pp[..., OFF_DB:N_IN].astype(F32).reshape(b_, l_, 2, DN_HEADS)
        g = -jnp.exp(a_log.astype(F32)) * jax.nn.softplus(a + dt_bias.astype(F32))
        return q, k, v, g.transpose(2, 0, 3, 1), jax.nn.sigmoid(bt).transpose(2, 0, 3, 1)

    def finish(o, pp):
        b_, l_ = pp.shape[:2]
        z = pp[..., OFF_DZ:OFF_DA].astype(F32).reshape(b_, l_, DN_HEADS, DN_DV)
        y = rmsnorm(o.transpose(0, 2, 1, 3), norm_w) * jax.nn.silu(z)
        return y.reshape(b_, l_, DN_HEADS * DN_DV).astype(pp.dtype)

    qx, kx, vx, gx, bx = prep(p)
    qc, kc, vc, gcc, bcc = prep(pc)
    s0 = jnp.zeros((p.shape[0], DN_HEADS, DN_DK, DN_DV), F32)
    ox = jnp.zeros_like(vx)
    oc = jnp.zeros_like(vc)
    for d in range(2):
        rev = d == 1
        o_c, s_c = gated_delta_chunked(_flip(qc, rev), _flip(kc, rev), _flip(vc, rev),
                                       _flip(gcc[d], rev), _flip(bcc[d], rev), s0)
        o_x, _ = gated_delta_chunked(_flip(qx, rev), _flip(kx, rev), _flip(vx, rev),
                                     _flip(gx[d], rev), _flip(bx[d], rev), s_c)
        ox = ox + _flip(o_x, rev)
        oc = oc + _flip(o_c, rev)
    y = finish(ox, p)
    yc = finish(oc, pc) if need_ctx else None
    return y, yc


def merge_branches(h, ys, w_gate, w_branch, w_o):
    acc = jax.nn.sigmoid(h @ w_gate[0]) * (ys[0] @ w_branch[0])
    for n in range(1, N_BRANCH):
        acc = acc + jax.nn.sigmoid(h @ w_gate[n]) * (ys[n] @ w_branch[n])
    return acc @ w_o


def sq_relu_mlp(h, w1, w2):
    return jnp.square(jax.nn.relu(h @ w1)) @ w2


def setup_inputs(seed: int = 0) -> dict:
    key = jax.random.key(seed)
    ks = jax.random.split(key, 32)

    def nrm(k, shape, scale):
        return jax.random.normal(k, shape, F32) * scale

    L = DEPTH
    dt = jnp.exp(jax.random.uniform(ks[22], (L, 2, DN_HEADS), F32, math.log(1e-3), math.log(1e-1)))
    return {
        'x': nrm(ks[0], (BATCH, SEQ, D_MODEL), 1.0),
        'c': nrm(ks[1], (BATCH, D_MODEL), 1.0),
        'ctx': nrm(ks[2], (BATCH, CTX_LEN, D_MODEL), 1.0),
        'c_ctx': nrm(ks[3], (D_MODEL,), 1.0),
        'w_ada': nrm(ks[4], (L, D_MODEL, 6 * D_MODEL), D_MODEL ** -0.5),
        'b_ada': nrm(ks[5], (L, 6 * D_MODEL), 0.02),
        'norm1': 1.0 + nrm(ks[6], (L, D_MODEL), 0.02),
        'norm2': 1.0 + nrm(ks[7], (L, D_MODEL), 0.02),
        'w_in': nrm(ks[8], (L, D_MODEL, N_IN), D_MODEL ** -0.5),
        'attn_qk_gain': 1.0 + nrm(ks[9], (L, 2, A_DQK), 0.02),
        'attn_lambda': nrm(ks[10], (L, 4, A_DQK), 0.1),
        'attn_subln': 1.0 + nrm(ks[11], (L, A_DV), 0.02),
        'hy_conv': nrm(ks[12], (L, HY_SHORT, (HY_ORDER + 1) * HY_W), HY_SHORT ** -0.5),
        'hy_w1': nrm(ks[13], (L, HY_EMB, HY_HID), HY_EMB ** -0.5),
        'hy_b1': nrm(ks[14], (L, HY_HID), 0.1),
        'hy_freq': 1.0 + nrm(ks[15], (L, 2, HY_HID), 0.05),
        'hy_w2': nrm(ks[16], (L, HY_HID, HY_HID), HY_HID ** -0.5),
        'hy_b2': nrm(ks[17], (L, HY_HID), 0.1),
        'hy_w3': nrm(ks[18], (L, HY_HID, HY_ORDER * 2 * HY_W), HY_HID ** -0.5),
        'hy_bias': nrm(ks[19], (L, HY_ORDER, HY_W), 0.1),
        'dn_conv': nrm(ks[20], (L, DN_SHORT, DN_HEADS * (2 * DN_DK + DN_DV)), DN_SHORT ** -0.5),
        'dn_a_log': jnp.log(jax.random.uniform(ks[21], (L, 2, DN_HEADS), F32, 1.0, 16.0)),
        'dn_dt_bias': dt + jnp.log(-jnp.expm1(-dt)),
        'dn_norm': 1.0 + nrm(ks[23], (L, DN_DV), 0.02),
        'w_gate': nrm(ks[24], (L, N_BRANCH, D_MODEL, D_MODEL), D_MODEL ** -0.5),
        'w_branch': nrm(ks[25], (L, N_BRANCH, BR_W, D_MODEL), BR_W ** -0.5),
        'w_o': nrm(ks[26], (L, D_MODEL, D_MODEL), D_MODEL ** -0.5),
        'w_mlp1': nrm(ks[27], (L, D_MODEL, D_FF), D_MODEL ** -0.5),
        'w_mlp2': nrm(ks[28], (L, D_FF, D_MODEL), D_FF ** -0.5),
    }


def reference(x, c, ctx, c_ctx, w_ada, b_ada, norm1, norm2, w_in, attn_qk_gain, attn_lambda,
              attn_subln, hy_conv, hy_w1, hy_b1, hy_freq, hy_w2, hy_b2, hy_w3, hy_bias,
              dn_conv, dn_a_log, dn_dt_bias, dn_norm, w_gate, w_branch, w_o, w_mlp1, w_mlp2):
    xc = ctx
    for li in range(DEPTH):
        need_ctx = li < DEPTH - 1
        lambda_init = 0.8 - 0.6 * math.exp(-0.3 * li)
        mod = (jax.nn.silu(c) @ w_ada[li] + b_ada[li])[:, None, :]
        mod_c = (jax.nn.silu(c_ctx) @ w_ada[li] + b_ada[li])[None, None, :]
        sh1, sc1, g1, sh2, sc2, g2 = jnp.split(mod, 6, axis=-1)
        sh1c, sc1c, g1c, sh2c, sc2c, g2c = jnp.split(mod_c, 6, axis=-1)

        h = rmsnorm(x, norm1[li]) * (1.0 + sc1) + sh1
        hc = rmsnorm(xc, norm1[li]) * (1.0 + sc1c) + sh1c
        p = h @ w_in[li]
        pc = hc @ w_in[li]

        ya, ya_c = diff_attention_branch(p, pc, attn_qk_gain[li], attn_lambda[li], attn_subln[li],
                                         lambda_init, need_ctx)
        yb = hyena_branch(p, hy_conv[li], hy_w1[li], hy_b1[li], hy_freq[li], hy_w2[li], hy_b2[li],
                          hy_w3[li], hy_bias[li])
        yf = fourier_branch(p)
        yd, yd_c = deltanet_branch(p, pc, dn_conv[li], dn_a_log[li], dn_dt_bias[li], dn_norm[li],
                                   need_ctx)

        x = x + g1 * merge_branches(h, (ya, yb, yf, yd), w_gate[li], w_branch[li], w_o[li])
        h2 = rmsnorm(x, norm2[li]) * (1.0 + sc2) + sh2
        x = x + g2 * sq_relu_mlp(h2, w_mlp1[li], w_mlp2[li])

        if need_ctx:
            yb_c = hyena_branch(pc, hy_conv[li], hy_w1[li], hy_b1[li], hy_freq[li], hy_w2[li],
                                hy_b2[li], hy_w3[li], hy_bias[li])
            yf_c = fourier_branch(pc)
            xc = xc + g1c * merge_branches(hc, (ya_c, yb_c, yf_c, yd_c), w_gate[li], w_branch[li], w_o[li])
            h2c = rmsnorm(xc, norm2[li]) * (1.0 + sc2c) + sh2c
            xc = xc + g2c * sq_relu_mlp(h2c, w_mlp1[li], w_mlp2[li])
    return x
```

```python
import functools
import math

import jax
import jax.numpy as jnp
from jax import lax
from jax.experimental import pallas as pl
from jax.experimental.pallas import tpu as pltpu

F32 = jnp.float32
BF16 = jnp.bfloat16

D_MODEL = 1024
BATCH = 16
SEQ = 2048
DEPTH = 4
GRID_W = 64
CTX_LEN = 256
N_BRANCH = 4
BR_W = 512
A_HEADS = 4
A_DQK = 64
A_DV = 2 * A_DQK
ROPE_BASE = 10000.0
HY_W = BR_W
HY_ORDER = 2
HY_SHORT = 3
HY_BANDS = 16
HY_HID = 64
HY_MIN_DECAY = math.log(1e-2) / 1.5
HY_MAX_DECAY = math.log(1e-2) / 0.3
FN_W = BR_W
FN_GROUPS = 4
DN_HEADS = 4
DN_DK = 128
DN_DV = 128
DN_CHUNK = 64
D_FF = 4 * D_MODEL
EPS = 1e-6

OFF_AQ = 0
OFF_AK = OFF_AQ + A_HEADS * 2 * A_DQK
OFF_AV = OFF_AK + A_HEADS * 2 * A_DQK
OFF_HY = OFF_AV + A_HEADS * A_DV
OFF_FN = OFF_HY + (HY_ORDER + 1) * HY_W
OFF_DN = OFF_FN + FN_W
OFF_DZ = OFF_DN + DN_HEADS * (2 * DN_DK + DN_DV)
OFF_DA = OFF_DZ + DN_HEADS * DN_DV
OFF_DB = OFF_DA + 2 * DN_HEADS
N_IN = OFF_DB + 2 * DN_HEADS

LANE = 128
N_IN_PAD = 45 * LANE
N_LAT = BATCH * SEQ
N_CTX = BATCH * CTX_LEN
N_ALL = N_LAT + N_CTX
MOD_ROWS = 8
VMEM_LIMIT = 56 * 1024 * 1024

ROW_TILE = 256
ATT_QT = 256


def _params(n_axes):
    return pltpu.CompilerParams(dimension_semantics=("arbitrary",) * n_axes,
                                vmem_limit_bytes=VMEM_LIMIT)


def _mod_row(i):
    return jnp.minimum(i // (SEQ // ROW_TILE), BATCH)


def _ada_norm(x, nw, shift, scale):
    ms = jnp.mean(x * x, axis=-1, keepdims=True)
    return (x * lax.rsqrt(ms + EPS) * nw) * (1.0 + scale) + shift


def _proj_in_kernel(x_ref, mod_ref, nw_ref, w_ref, p_ref):
    h = _ada_norm(x_ref[...], nw_ref[...], mod_ref[0, 0:1, :], mod_ref[0, 1:2, :])
    p_ref[...] = jnp.dot(h.astype(BF16), w_ref[...], preferred_element_type=F32)


def _proj_in(x_all, mod, nw, w_in_pad):
    n_tiles = N_ALL // ROW_TILE
    return pl.pallas_call(
        _proj_in_kernel,
        grid=(n_tiles,),
        in_specs=[
            pl.BlockSpec((ROW_TILE, D_MODEL), lambda i: (i, 0)),
            pl.BlockSpec((1, MOD_ROWS, D_MODEL), lambda i: (_mod_row(i), 0, 0)),
            pl.BlockSpec((1, D_MODEL), lambda i: (0, 0)),
            pl.BlockSpec((D_MODEL, N_IN_PAD), lambda i: (0, 0)),
        ],
        out_specs=pl.BlockSpec((ROW_TILE, N_IN_PAD), lambda i: (i, 0)),
        out_shape=jax.ShapeDtypeStruct((N_ALL, N_IN_PAD), F32),
        compiler_params=_params(1),
        name="proj_in",
    )(x_all, mod, nw, w_in_pad)


def _merge_kernel(x_ref, mod_ref, nw_ref, ya_ref, yb_ref, yf_ref, yd_ref, wg_ref, wb_ref, wo_ref,
                  o_ref):
    x = x_ref[...]
    h = _ada_norm(x, nw_ref[...], mod_ref[0, 0:1, :], mod_ref[0, 1:2, :]).astype(BF16)
    acc = None
    for n, y_ref in enumerate((ya_ref, yb_ref, yf_ref, yd_ref)):
        gate = jax.nn.sigmoid(jnp.dot(h, wg_ref[n], preferred_element_type=F32))
        br = jnp.dot(y_ref[...].astype(BF16), wb_ref[n], preferred_element_type=F32)
        acc = gate * br if acc is None else acc + gate * br
    out = jnp.dot(acc.astype(BF16), wo_ref[...], preferred_element_type=F32)
    o_ref[...] = x + mod_ref[0, 2:3, :] * out


def _merge(x_all, mod, nw, ys, wg, wb, wo, n_rows):
    row = lambda i: (i, 0)
    const2 = lambda i: (0, 0)
    const3 = lambda i: (0, 0, 0)
    return pl.pallas_call(
        _merge_kernel,
        grid=(n_rows // ROW_TILE,),
        in_specs=[
            pl.BlockSpec((ROW_TILE, D_MODEL), row),
            pl.BlockSpec((1, MOD_ROWS, D_MODEL), lambda i: (_mod_row(i), 0, 0)),
            pl.BlockSpec((1, D_MODEL), const2),
        ] + [pl.BlockSpec((ROW_TILE, BR_W), row)] * N_BRANCH + [
            pl.BlockSpec((N_BRANCH, D_MODEL, D_MODEL), const3),
            pl.BlockSpec((N_BRANCH, BR_W, D_MODEL), const3),
            pl.BlockSpec((D_MODEL, D_MODEL), const2),
        ],
        out_specs=pl.BlockSpec((ROW_TILE, D_MODEL), row),
        out_shape=jax.ShapeDtypeStruct((N_ALL, D_MODEL), F32),
        input_output_aliases={0: 0},
        compiler_params=_params(1),
        name="merge",
    )(x_all, mod, nw, *ys, wg, wb, wo)


def _mlp_kernel(x_ref, mod_ref, nw_ref, w1_ref, w2_ref, o_ref):
    x = x_ref[...]
    h = _ada_norm(x, nw_ref[...], mod_ref[0, 3:4, :], mod_ref[0, 4:5, :]).astype(BF16)
    u = jnp.maximum(jnp.dot(h, w1_ref[...], preferred_element_type=F32), 0.0)
    out = jnp.dot((u * u).astype(BF16), w2_ref[...], preferred_element_type=F32)
    o_ref[...] = x + mod_ref[0, 5:6, :] * out


def _mlp(x_all, mod, nw, w1, w2, n_rows):
    row = lambda i: (i, 0)
    const2 = lambda i: (0, 0)
    return pl.pallas_call(
        _mlp_kernel,
        grid=(n_rows // ROW_TILE,),
        in_specs=[
            pl.BlockSpec((ROW_TILE, D_MODEL), row),
            pl.BlockSpec((1, MOD_ROWS, D_MODEL), lambda i: (_mod_row(i), 0, 0)),
            pl.BlockSpec((1, D_MODEL), const2),
            pl.BlockSpec((D_MODEL, D_FF), const2),
            pl.BlockSpec((D_FF, D_MODEL), const2),
        ],
        out_specs=pl.BlockSpec((ROW_TILE, D_MODEL), row),
        out_shape=jax.ShapeDtypeStruct((N_ALL, D_MODEL), F32),
        input_output_aliases={0: 0},
        compiler_params=_params(1),
        name="mlp",
    )(x_all, mod, nw, w1, w2)


def _qk_norm(x, gain):
    lane = lax.broadcasted_iota(jnp.int32, x.shape, 1)
    first = lane < A_DQK
    sq = x * x
    s_all = jnp.sum(sq, axis=-1, keepdims=True)
    s_first = jnp.sum(jnp.where(first, sq, 0.0), axis=-1, keepdims=True)
    ms = jnp.where(first, s_first, s_all - s_first) * (1.0 / A_DQK)
    return x * lax.rsqrt(ms + EPS) * gain


def _rope(x, cos, sin_lo, sin_hi):
    up = pltpu.roll(x, LANE - A_DQK // 4, 1)
    down = pltpu.roll(x, A_DQK // 4, 1)
    return x * cos + up * sin_lo + down * sin_hi


def _attn_kernel(lam_ref, qg_ref, kg_ref, sub_ref, q_ref, *rest, with_latent, lambda_init):
    if with_latent:
        (kc_ref, vc_ref, kl_ref, vl_ref, cq_ref, sq_lo_ref, sq_hi_ref,
         ck_ref, sk_lo_ref, sk_hi_ref, o_ref, kn_ref, vn_ref) = rest
    else:
        kc_ref, vc_ref, o_ref, kn_ref, vn_ref = rest

    @pl.when(pl.program_id(2) == 0)
    def _prep_keys():
        kn_ref[0:CTX_LEN, :] = _qk_norm(kc_ref[...], kg_ref[...]).astype(BF16)
        vn_ref[0:CTX_LEN, :] = vc_ref[...].astype(BF16)
        if with_latent:
            kl = _qk_norm(kl_ref[...], kg_ref[...])
            kl = _rope(kl, ck_ref[...], sk_lo_ref[...], sk_hi_ref[...])
            kn_ref[CTX_LEN:, :] = kl.astype(BF16)
            vn_ref[CTX_LEN:, :] = vl_ref[...].astype(BF16)

    q = _qk_norm(q_ref[...], qg_ref[...])
    if with_latent:
        q = _rope(q, cq_ref[...], sq_lo_ref[...], sq_hi_ref[...])
    q = q * (A_DQK ** -0.5)
    lane = lax.broadcasted_iota(jnp.int32, q.shape, 1)
    first = lane < A_DQK
    kn = kn_ref[...]
    vn = vn_ref[...]

    def component(qc):
        s = lax.dot_general(qc.astype(BF16), kn, (((1,), (1,)), ((), ())),
                            preferred_element_type=F32)
        m = jnp.max(s, axis=-1, keepdims=True)
        e = jnp.exp(s - m)
        l = jnp.sum(e, axis=-1, keepdims=True)
        return jnp.dot(e.astype(BF16), vn, preferred_element_type=F32) / l

    lf = lam_ref[...]
    lam = (jnp.exp(jnp.sum(lf[0:1] * lf[1:2], axis=-1, keepdims=True))
           - jnp.exp(jnp.sum(lf[2:3] * lf[3:4], axis=-1, keepdims=True)) + lambda_init)
    o = component(jnp.where(first, q, 0.0)) - lam * component(jnp.where(first, 0.0, q))
    ms = jnp.mean(o * o, axis=-1, keepdims=True)
    o_ref[...] = (o * lax.rsqrt(ms + EPS) * sub_ref[...]) * (1.0 - lambda_init)


def _attention(p, lam_vec, q_gain, k_gain, subln, rope_tabs, lambda_init, with_latent):
    kcol = OFF_AK // LANE
    vcol = OFF_AV // LANE
    ctx_blk0 = N_LAT // CTX_LEN
    small = lambda b, h, i: (0, 0)
    if with_latent:
        qt, nq = ATT_QT, SEQ // ATT_QT
        q_spec = pl.BlockSpec((qt, LANE), lambda b, h, i: (b * nq + i, h))
        n_keys = CTX_LEN + SEQ
    else:
        qt, nq = CTX_LEN, 1
        q_spec = pl.BlockSpec((qt, LANE), lambda b, h, i: (ctx_blk0 + b, h))
        n_keys = CTX_LEN
    in_specs = [
        pl.BlockSpec((4, A_DQK), small),
        pl.BlockSpec((1, LANE), small),
        pl.BlockSpec((1, LANE), small),
        pl.BlockSpec((1, LANE), small),
        q_spec,
        pl.BlockSpec((CTX_LEN, LANE), lambda b, h, i: (ctx_blk0 + b, kcol + h)),
        pl.BlockSpec((CTX_LEN, LANE), lambda b, h, i: (ctx_blk0 + b, vcol + h)),
    ]
    args = [lam_vec, q_gain, k_gain, subln, p, p, p]
    if with_latent:
        in_specs += [
            pl.BlockSpec((SEQ, LANE), lambda b, h, i: (b, kcol + h)),
            pl.BlockSpec((SEQ, LANE), lambda b, h, i: (b, vcol + h)),
        ] + [pl.BlockSpec((qt, LANE), lambda b, h, i: (i, 0))] * 3 + [
            pl.BlockSpec((SEQ, LANE), small)] * 3
        args += [p, p] + list(rope_tabs) + list(rope_tabs)
    n_rows = N_LAT if with_latent else N_CTX
    return pl.pallas_call(
        functools.partial(_attn_kernel, with_latent=with_latent, lambda_init=lambda_init),
        grid=(BATCH, A_HEADS, nq),
        in_specs=in_specs,
        out_specs=pl.BlockSpec((qt, LANE), lambda b, h, i: (b * nq + i, h)),
        out_shape=jax.ShapeDtypeStruct((n_rows, A_HEADS * A_DV), F32),
        scratch_shapes=[pltpu.VMEM((n_keys, LANE), BF16), pltpu.VMEM((n_keys, LANE), BF16)],
        compiler_params=_params(3),
        name="diff_attn_latent" if with_latent else "diff_attn_ctx",
    )(*args)


def _rope_tables():
    t = jnp.arange(SEQ, dtype=jnp.int32)
    pos = jnp.stack([(t // GRID_W).astype(F32), (t % GRID_W).astype(F32)], axis=1)
    n_freq = A_DQK // 4
    inv = ROPE_BASE ** (-jnp.arange(n_freq, dtype=F32) / n_freq)
    ang = pos[:, :, None] * inv
    cos = jnp.cos(ang)[:, None, :, None, :]
    sin = jnp.sin(ang)[:, None, :, None, :]
    shape = (SEQ, 2, 2, 2, n_freq)
    half = jnp.arange(2).reshape(1, 1, 1, 2, 1)
    cos_t = jnp.broadcast_to(cos, shape).reshape(SEQ, LANE)
    sin_lo = jnp.where(half == 0, -jnp.broadcast_to(sin, shape), 0.0).reshape(SEQ, LANE)
    sin_hi = jnp.where(half == 1, jnp.broadcast_to(sin, shape), 0.0).reshape(SEQ, LANE)
    return cos_t, sin_lo, sin_hi


def _centred_conv(u, w):
    k = w.shape[0]
    r = k // 2
    length = u.shape[1]
    up = jnp.pad(u, ((0, 0), (r, r), (0, 0)))
    out = up[:, 0:length] * w[0]
    for i in range(1, k):
        out = out + up[:, i:i + length] * w[i]
    return out


def _hyena_filter_spectra(length, w1, b1, freq, w2, b2, w3):
    t = jnp.linspace(0.0, 1.0, length, dtype=F32)[:, None]
    w = 2.0 * math.pi * jnp.arange(length, dtype=F32)[:, None] / length
    bands = jnp.linspace(1e-4, HY_BANDS - 1, HY_BANDS, dtype=F32)[None]
    feats = jnp.concatenate([t, jnp.cos(bands * w), -jnp.sin(bands * w)], axis=-1)
    hdn = jnp.sin(freq[0] * (feats @ w1 + b1))
    hdn = jnp.sin(freq[1] * (hdn @ w2 + b2))
    h = (hdn @ w3).reshape(length, HY_ORDER, 2, HY_W)
    deltas = jnp.abs(jnp.linspace(HY_MIN_DECAY, HY_MAX_DECAY, HY_W, dtype=F32))
    h = h * jnp.exp(-t[:, :, None, None] * deltas)
    taps = jnp.concatenate([h[:, :, 0], jnp.zeros((1, HY_ORDER, HY_W), F32), h[:0:-1, :, 1]], axis=0)
    taps = taps / jnp.sum(jnp.abs(taps), axis=0, keepdims=True)
    return jnp.fft.rfft(taps, n=2 * length, axis=0)


def _hyena_jnp(pp, conv_w, w1, b1, freq, w2, b2, w3, bias):
    length = pp.shape[1]
    u = _centred_conv(pp[..., OFF_HY:OFF_FN], conv_w)
    parts = jnp.split(u, HY_ORDER + 1, axis=-1)
    kf = _hyena_filter_spectra(length, w1, b1, freq, w2, b2, w3)
    z = parts[-1]
    for n in range(HY_ORDER):
        zf = jnp.fft.rfft(z, n=2 * length, axis=1)
        z = parts[n] * (jnp.fft.irfft(zf * kf[:, n], n=2 * length, axis=1)[:, :length] + bias[n] * z)
    return z


def _fnet_jnp(pp):
    b, l, _ = pp.shape
    u = pp[..., OFF_FN:OFF_DN].reshape(b, l, FN_GROUPS, FN_W // FN_GROUPS)
    return jnp.fft.fft2(u, axes=(1, 3), norm='ortho').real.reshape(b, l, FN_W)


def _l2norm(x):
    return x * lax.rsqrt(jnp.sum(x * x, axis=-1, keepdims=True) + EPS)


def _gdn_chunked_jnp(q, k, v, g, beta, s0):
    b, h, l, dk = q.shape
    dv = v.shape[-1]
    cs = DN_CHUNK
    n = l // cs
    q = (q * dk ** -0.5).reshape(b, h, n, cs, dk)
    k = k.reshape(b, h, n, cs, dk)
    v = v.reshape(b, h, n, cs, dv)
    beta = beta.reshape(b, h, n, cs)
    gc = jnp.cumsum(g.reshape(b, h, n, cs), axis=-1)
    idx = jnp.arange(cs)
    incl = idx[:, None] >= idx[None, :]
    strict = idx[:, None] > idx[None, :]
    diff = gc[..., :, None] - gc[..., None, :]
    decay = jnp.where(incl, jnp.exp(jnp.where(incl, diff, 0.0)), 0.0)
    kb = k * beta[..., None]
    a_mat = jnp.where(strict, jnp.einsum('bhncd,bhnsd->bhncs', kb, k) * decay, 0.0)
    eye = jnp.eye(cs, dtype=q.dtype)
    t_mat = lax.linalg.triangular_solve(a_mat + eye, jnp.broadcast_to(eye, a_mat.shape),
                                        left_side=True, lower=True, unit_diagonal=True)
    u_c = t_mat @ (v * beta[..., None])
    w_c = t_mat @ (kb * jnp.exp(gc)[..., None])
    intra = jnp.where(incl, jnp.einsum('bhncd,bhnsd->bhncs', q, k) * decay, 0.0)
    q_dec = q * jnp.exp(gc)[..., None]
    k_dec = k * jnp.exp(gc[..., -1:] - gc)[..., None]
    g_last = jnp.exp(gc[..., -1])
    xs = tuple(jnp.moveaxis(a, 2, 0) for a in (u_c, w_c, intra, q_dec, k_dec, g_last))

    def step(s, inp):
        u_i, w_i, a_i, qd_i, kd_i, gl_i = inp
        v_new = u_i - jnp.einsum('bhck,bhkv->bhcv', w_i, s)
        o_i = jnp.einsum('bhck,bhkv->bhcv', qd_i, s) + jnp.einsum('bhcs,bhsv->bhcv', a_i, v_new)
        s = s * gl_i[..., None, None] + jnp.einsum('bhck,bhcv->bhkv', kd_i, v_new)
        return s, o_i

    s_fin, o = lax.scan(step, s0, xs)
    return jnp.moveaxis(o, 0, 2).reshape(b, h, l, dv), s_fin


def _deltanet_jnp(p, pc, conv_w, a_log, dt_bias, norm_w, need_ctx):
    def prep(pp):
        b_, l_, _ = pp.shape
        qkv = jax.nn.silu(_centred_conv(pp[..., OFF_DN:OFF_DZ], conv_w))
        q, k, v = jnp.split(qkv, [DN_HEADS * DN_DK, 2 * DN_HEADS * DN_DK], axis=-1)
        q = _l2norm(q.reshape(b_, l_, DN_HEADS, DN_DK)).transpose(0, 2, 1, 3)
        k = _l2norm(k.reshape(b_, l_, DN_HEADS, DN_DK)).transpose(0, 2, 1, 3)
        v = v.reshape(b_, l_, DN_HEADS, DN_DV).transpose(0, 2, 1, 3)
        a = pp[..., OFF_DA:OFF_DB].reshape(b_, l_, 2, DN_HEADS)
        bt = pp[..., OFF_DB:N_IN].reshape(b_, l_, 2, DN_HEADS)
        g = -jnp.exp(a_log) * jax.nn.softplus(a + dt_bias)
        return q, k, v, g.transpose(2, 0, 3, 1), jax.nn.sigmoid(bt).transpose(2, 0, 3, 1)

    def finish(o, pp):
        b_, l_ = pp.shape[:2]
        z = pp[..., OFF_DZ:OFF_DA].reshape(b_, l_, DN_HEADS, DN_DV)
        ot = o.transpose(0, 2, 1, 3)
        y = ot * lax.rsqrt(jnp.mean(ot * ot, axis=-1, keepdims=True) + EPS) * norm_w
        return (y * jax.nn.silu(z)).reshape(b_, l_, DN_HEADS * DN_DV)

    flip = lambda t, rev: jnp.flip(t, axis=2) if rev else t
    qx, kx, vx, gx, bx = prep(p)
    qc, kc, vc, gcc, bcc = prep(pc)
    s0 = jnp.zeros((p.shape[0], DN_HEADS, DN_DK, DN_DV), F32)
    ox = jnp.zeros_like(vx)
    oc = jnp.zeros_like(vc)
    for d in range(2):
        rev = d == 1
        o_c, s_c = _gdn_chunked_jnp(flip(qc, rev), flip(kc, rev), flip(vc, rev),
                                    flip(gcc[d], rev), flip(bcc[d], rev), s0)
        o_x, _ = _gdn_chunked_jnp(flip(qx, rev), flip(kx, rev), flip(vx, rev),
                                  flip(gx[d], rev), flip(bx[d], rev), s_c)
        ox = ox + flip(o_x, rev)
        oc = oc + flip(o_c, rev)
    return finish(ox, p), (finish(oc, pc) if need_ctx else None)


def kernel(x, c, ctx, c_ctx, w_ada, b_ada, norm1, norm2, w_in, attn_qk_gain, attn_lambda, attn_subln,
           hy_conv, hy_w1, hy_b1, hy_freq, hy_w2, hy_b2, hy_w3, hy_bias, dn_conv, dn_a_log,
           dn_dt_bias, dn_norm, w_gate, w_branch, w_o, w_mlp1, w_mlp2):
    x_all = jnp.concatenate([x.reshape(N_LAT, D_MODEL), ctx.reshape(N_CTX, D_MODEL)], axis=0)
    cond = jax.nn.silu(jnp.concatenate([c, c_ctx[None, :]], axis=0))
    rope_tabs = _rope_tables()

    for li in range(DEPTH):
        need_ctx = li < DEPTH - 1
        lambda_init = 0.8 - 0.6 * math.exp(-0.3 * li)
        mod = (cond @ w_ada[li] + b_ada[li]).reshape(BATCH + 1, 6, D_MODEL)
        mod = jnp.pad(mod, ((0, 0), (0, MOD_ROWS - 6), (0, 0)))
        w_in_pad = jnp.pad(w_in[li], ((0, 0), (0, N_IN_PAD - N_IN))).astype(BF16)

        p_all = _proj_in(x_all, mod, norm1[li][None, :], w_in_pad)

        q_gain = jnp.tile(attn_qk_gain[li, 0], 2)[None, :]
        k_gain = jnp.tile(attn_qk_gain[li, 1], 2)[None, :]
        subln = attn_subln[li][None, :]
        ya = _attention(p_all, attn_lambda[li], q_gain, k_gain, subln, rope_tabs, lambda_init, True)

        p = p_all[:N_LAT, :N_IN].reshape(BATCH, SEQ, N_IN)
        pc = p_all[N_LAT:, :N_IN].reshape(BATCH, CTX_LEN, N_IN)
        hy_args = (hy_conv[li], hy_w1[li], hy_b1[li], hy_freq[li], hy_w2[li], hy_b2[li], hy_w3[li],
                   hy_bias[li])
        yb = _hyena_jnp(p, *hy_args).reshape(N_LAT, BR_W)
        yf = _fnet_jnp(p).reshape(N_LAT, BR_W)
        yd, yd_c = _deltanet_jnp(p, pc, dn_conv[li], dn_a_log[li], dn_dt_bias[li], dn_norm[li], need_ctx)
        yd = yd.reshape(N_LAT, BR_W)
        ys = [ya, yb, yf, yd]
        n_rows = N_LAT
        if need_ctx:
            ya_c = _attention(p_all, attn_lambda[li], q_gain, k_gain, subln, None, lambda_init, False)
            yb_c = _hyena_jnp(pc, *hy_args).reshape(N_CTX, BR_W)
            yf_c = _fnet_jnp(pc).reshape(N_CTX, BR_W)
            ys = [jnp.concatenate([a, b], axis=0)
                  for a, b in zip(ys, (ya_c, yb_c, yf_c, yd_c.reshape(N_CTX, BR_W)))]
            n_rows = N_ALL

        x_all = _merge(x_all, mod, norm1[li][None, :], ys, w_gate[li].astype(BF16),
                       w_branch[li].astype(BF16), w_o[li].astype(BF16), n_rows)
        x_all = _mlp(x_all, mod, norm2[li][None, :], w_mlp1[li].astype(BF16),
                     w_mlp2[li].astype(BF16), n_rows)
    return x_all[:N_LAT].reshape(BATCH, SEQ, D_MODEL)
```

```python
import functools
import math

import jax
import jax.numpy as jnp
from jax import lax
from jax.experimental import pallas as pl
from jax.experimental.pallas import tpu as pltpu

F32 = jnp.float32
BF16 = jnp.bfloat16

D_MODEL = 1024
BATCH = 16
SEQ = 2048
DEPTH = 4
GRID_W = 64
CTX_LEN = 256
N_BRANCH = 4
BR_W = 512
A_HEADS = 4
A_DQK = 64
A_DV = 2 * A_DQK
ROPE_BASE = 10000.0
HY_W = BR_W
HY_ORDER = 2
HY_SHORT = 3
HY_BANDS = 16
HY_HID = 64
HY_MIN_DECAY = math.log(1e-2) / 1.5
HY_MAX_DECAY = math.log(1e-2) / 0.3
FN_W = BR_W
FN_GROUPS = 4
DN_HEADS = 4
DN_DK = 128
DN_DV = 128
DN_CHUNK = 64
D_FF = 4 * D_MODEL
EPS = 1e-6

OFF_AQ = 0
OFF_AK = OFF_AQ + A_HEADS * 2 * A_DQK
OFF_AV = OFF_AK + A_HEADS * 2 * A_DQK
OFF_HY = OFF_AV + A_HEADS * A_DV
OFF_FN = OFF_HY + (HY_ORDER + 1) * HY_W
OFF_DN = OFF_FN + FN_W
OFF_DZ = OFF_DN + DN_HEADS * (2 * DN_DK + DN_DV)
OFF_DA = OFF_DZ + DN_HEADS * DN_DV
OFF_DB = OFF_DA + 2 * DN_HEADS
N_IN = OFF_DB + 2 * DN_HEADS

LANE = 128
N_IN_PAD = 45 * LANE
N_LAT = BATCH * SEQ
N_CTX = BATCH * CTX_LEN
N_ALL = N_LAT + N_CTX
MOD_ROWS = 8
VMEM_LIMIT = 56 * 1024 * 1024

ROW_TILE = 256
ATT_QT = 256


def _params(n_axes):
    return pltpu.CompilerParams(dimension_semantics=("arbitrary",) * n_axes,
                                vmem_limit_bytes=VMEM_LIMIT)


def _mod_row(i):
    return jnp.minimum(i // (SEQ // ROW_TILE), BATCH)


def _ada_norm(x, nw, shift, scale):
    ms = jnp.mean(x * x, axis=-1, keepdims=True)
    return (x * lax.rsqrt(ms + EPS) * nw) * (1.0 + scale) + shift


def _proj_in_kernel(x_ref, mod_ref, nw_ref, w_ref, p_ref):
    h = _ada_norm(x_ref[...], nw_ref[...], mod_ref[0, 0:1, :], mod_ref[0, 1:2, :])
    p_ref[...] = jnp.dot(h.astype(BF16), w_ref[...], preferred_element_type=F32)


def _proj_in(x_all, mod, nw, w_in_pad):
    n_tiles = N_ALL // ROW_TILE
    return pl.pallas_call(
        _proj_in_kernel,
        grid=(n_tiles,),
        in_specs=[
            pl.BlockSpec((ROW_TILE, D_MODEL), lambda i: (i, 0)),
            pl.BlockSpec((1, MOD_ROWS, D_MODEL), lambda i: (_mod_row(i), 0, 0)),
            pl.BlockSpec((1, D_MODEL), lambda i: (0, 0)),
            pl.BlockSpec((D_MODEL, N_IN_PAD), lambda i: (0, 0)),
        ],
        out_specs=pl.BlockSpec((ROW_TILE, N_IN_PAD), lambda i: (i, 0)),
        out_shape=jax.ShapeDtypeStruct((N_ALL, N_IN_PAD), F32),
        compiler_params=_params(1),
        name="proj_in",
    )(x_all, mod, nw, w_in_pad)


def _merge_kernel(x_ref, mod_ref, nw_ref, ya_ref, yb_ref, yf_ref, yd_ref, wg_ref, wb_ref, wo_ref,
                  o_ref):
    x = x_ref[...]
    h = _ada_norm(x, nw_ref[...], mod_ref[0, 0:1, :], mod_ref[0, 1:2, :]).astype(BF16)
    acc = None
    for n, y_ref in enumerate((ya_ref, yb_ref, yf_ref, yd_ref)):
        gate = jax.nn.sigmoid(jnp.dot(h, wg_ref[n], preferred_element_type=F32))
        br = jnp.dot(y_ref[...].astype(BF16), wb_ref[n], preferred_element_type=F32)
        acc = gate * br if acc is None else acc + gate * br
    out = jnp.dot(acc.astype(BF16), wo_ref[...], preferred_element_type=F32)
    o_ref[...] = x + mod_ref[0, 2:3, :] * out


def _merge(x_all, mod, nw, ys, wg, wb, wo, row_blk0, n_rows):
    xrow = lambda i: (row_blk0 + i, 0)
    yrow = lambda i: (i, 0)
    const2 = lambda i: (0, 0)
    const3 = lambda i: (0, 0, 0)
    return pl.pallas_call(
        _merge_kernel,
        grid=(n_rows // ROW_TILE,),
        in_specs=[
            pl.BlockSpec((ROW_TILE, D_MODEL), xrow),
            pl.BlockSpec((1, MOD_ROWS, D_MODEL), lambda i: (_mod_row(row_blk0 + i), 0, 0)),
            pl.BlockSpec((1, D_MODEL), const2),
        ] + [pl.BlockSpec((ROW_TILE, BR_W), yrow)] * N_BRANCH + [
            pl.BlockSpec((N_BRANCH, D_MODEL, D_MODEL), const3),
            pl.BlockSpec((N_BRANCH, BR_W, D_MODEL), const3),
            pl.BlockSpec((D_MODEL, D_MODEL), const2),
        ],
        out_specs=pl.BlockSpec((ROW_TILE, D_MODEL), xrow),
        out_shape=jax.ShapeDtypeStruct((N_ALL, D_MODEL), F32),
        input_output_aliases={0: 0},
        compiler_params=_params(1),
        name="merge",
    )(x_all, mod, nw, *ys, wg, wb, wo)


def _mlp_kernel(x_ref, mod_ref, nw_ref, w1_ref, w2_ref, o_ref):
    x = x_ref[...]
    h = _ada_norm(x, nw_ref[...], mod_ref[0, 3:4, :], mod_ref[0, 4:5, :]).astype(BF16)
    u = jnp.maximum(jnp.dot(h, w1_ref[...], preferred_element_type=F32), 0.0)
    out = jnp.dot((u * u).astype(BF16), w2_ref[...], preferred_element_type=F32)
    o_ref[...] = x + mod_ref[0, 5:6, :] * out


def _mlp(x_all, mod, nw, w1, w2, n_rows):
    row = lambda i: (i, 0)
    const2 = lambda i: (0, 0)
    return pl.pallas_call(
        _mlp_kernel,
        grid=(n_rows // ROW_TILE,),
        in_specs=[
            pl.BlockSpec((ROW_TILE, D_MODEL), row),
            pl.BlockSpec((1, MOD_ROWS, D_MODEL), lambda i: (_mod_row(i), 0, 0)),
            pl.BlockSpec((1, D_MODEL), const2),
            pl.BlockSpec((D_MODEL, D_FF), const2),
            pl.BlockSpec((D_FF, D_MODEL), const2),
        ],
        out_specs=pl.BlockSpec((ROW_TILE, D_MODEL), row),
        out_shape=jax.ShapeDtypeStruct((N_ALL, D_MODEL), F32),
        input_output_aliases={0: 0},
        compiler_params=_params(1),
        name="mlp",
    )(x_all, mod, nw, w1, w2)


def _qk_norm(x, gain):
    lane = lax.broadcasted_iota(jnp.int32, x.shape, 1)
    first = lane < A_DQK
    sq = x * x
    s_all = jnp.sum(sq, axis=-1, keepdims=True)
    s_first = jnp.sum(jnp.where(first, sq, 0.0), axis=-1, keepdims=True)
    ms = jnp.where(first, s_first, s_all - s_first) * (1.0 / A_DQK)
    return x * lax.rsqrt(ms + EPS) * gain


def _rope(x, cos, sin_lo, sin_hi):
    up = pltpu.roll(x, LANE - A_DQK // 4, 1)
    down = pltpu.roll(x, A_DQK // 4, 1)
    return x * cos + up * sin_lo + down * sin_hi


def _attn_kernel(lam_ref, qg_ref, kg_ref, sub_ref, q_ref, *rest, with_latent, lambda_init):
    if with_latent:
        (kc_ref, vc_ref, kl_ref, vl_ref, cq_ref, sq_lo_ref, sq_hi_ref,
         ck_ref, sk_lo_ref, sk_hi_ref, o_ref, kn_ref, vn_ref) = rest
    else:
        kc_ref, vc_ref, o_ref, kn_ref, vn_ref = rest

    @pl.when(pl.program_id(2) == 0)
    def _prep_keys():
        kn_ref[0:CTX_LEN, :] = _qk_norm(kc_ref[...], kg_ref[...]).astype(BF16)
        vn_ref[0:CTX_LEN, :] = vc_ref[...].astype(BF16)
        if with_latent:
            kl = _qk_norm(kl_ref[...], kg_ref[...])
            kl = _rope(kl, ck_ref[...], sk_lo_ref[...], sk_hi_ref[...])
            kn_ref[CTX_LEN:, :] = kl.astype(BF16)
            vn_ref[CTX_LEN:, :] = vl_ref[...].astype(BF16)

    q = _qk_norm(q_ref[...], qg_ref[...])
    if with_latent:
        q = _rope(q, cq_ref[...], sq_lo_ref[...], sq_hi_ref[...])
    q = q * (A_DQK ** -0.5)
    lane = lax.broadcasted_iota(jnp.int32, q.shape, 1)
    first = lane < A_DQK
    kn = kn_ref[...]
    vn = vn_ref[...]

    def component(qc):
        s = lax.dot_general(qc.astype(BF16), kn, (((1,), (1,)), ((), ())),
                            preferred_element_type=F32)
        m = jnp.max(s, axis=-1, keepdims=True)
        e = jnp.exp(s - m)
        l = jnp.sum(e, axis=-1, keepdims=True)
        return jnp.dot(e.astype(BF16), vn, preferred_element_type=F32) / l

    lf = lam_ref[...]
    lam = (jnp.exp(jnp.sum(lf[0:1] * lf[1:2], axis=-1, keepdims=True))
           - jnp.exp(jnp.sum(lf[2:3] * lf[3:4], axis=-1, keepdims=True)) + lambda_init)
    o = component(jnp.where(first, q, 0.0)) - lam * component(jnp.where(first, 0.0, q))
    ms = jnp.mean(o * o, axis=-1, keepdims=True)
    o_ref[...] = (o * lax.rsqrt(ms + EPS) * sub_ref[...]) * (1.0 - lambda_init)


def _attention(p, lam_vec, q_gain, k_gain, subln, rope_tabs, lambda_init, with_latent):
    kcol = OFF_AK // LANE
    vcol = OFF_AV // LANE
    ctx_blk0 = N_LAT // CTX_LEN
    small = lambda b, h, i: (0, 0)
    if with_latent:
        qt, nq = ATT_QT, SEQ // ATT_QT
        q_spec = pl.BlockSpec((qt, LANE), lambda b, h, i: (b * nq + i, h))
        n_keys = CTX_LEN + SEQ
    else:
        qt, nq = CTX_LEN, 1
        q_spec = pl.BlockSpec((qt, LANE), lambda b, h, i: (ctx_blk0 + b, h))
        n_keys = CTX_LEN
    in_specs = [
        pl.BlockSpec((4, A_DQK), small),
        pl.BlockSpec((1, LANE), small),
        pl.BlockSpec((1, LANE), small),
        pl.BlockSpec((1, LANE), small),
        q_spec,
        pl.BlockSpec((CTX_LEN, LANE), lambda b, h, i: (ctx_blk0 + b, kcol + h)),
        pl.BlockSpec((CTX_LEN, LANE), lambda b, h, i: (ctx_blk0 + b, vcol + h)),
    ]
    args = [lam_vec, q_gain, k_gain, subln, p, p, p]
    if with_latent:
        in_specs += [
            pl.BlockSpec((SEQ, LANE), lambda b, h, i: (b, kcol + h)),
            pl.BlockSpec((SEQ, LANE), lambda b, h, i: (b, vcol + h)),
        ] + [pl.BlockSpec((qt, LANE), lambda b, h, i: (i, 0))] * 3 + [
            pl.BlockSpec((SEQ, LANE), small)] * 3
        args += [p, p] + list(rope_tabs) + list(rope_tabs)
    n_rows = N_LAT if with_latent else N_CTX
    return pl.pallas_call(
        functools.partial(_attn_kernel, with_latent=with_latent, lambda_init=lambda_init),
        grid=(BATCH, A_HEADS, nq),
        in_specs=in_specs,
        out_specs=pl.BlockSpec((qt, LANE), lambda b, h, i: (b * nq + i, h)),
        out_shape=jax.ShapeDtypeStruct((n_rows, A_HEADS * A_DV), F32),
        scratch_shapes=[pltpu.VMEM((n_keys, LANE), BF16), pltpu.VMEM((n_keys, LANE), BF16)],
        compiler_params=_params(3),
        name="diff_attn_latent" if with_latent else "diff_attn_ctx",
    )(*args)


def _rope_tables():
    t = jnp.arange(SEQ, dtype=jnp.int32)
    pos = jnp.stack([(t // GRID_W).astype(F32), (t % GRID_W).astype(F32)], axis=1)
    n_freq = A_DQK // 4
    inv = ROPE_BASE ** (-jnp.arange(n_freq, dtype=F32) / n_freq)
    ang = pos[:, :, None] * inv
    cos = jnp.cos(ang)[:, None, :, None, :]
    sin = jnp.sin(ang)[:, None, :, None, :]
    shape = (SEQ, 2, 2, 2, n_freq)
    half = jnp.arange(2).reshape(1, 1, 1, 2, 1)
    cos_t = jnp.broadcast_to(cos, shape).reshape(SEQ, LANE)
    sin_lo = jnp.where(half == 0, -jnp.broadcast_to(sin, shape), 0.0).reshape(SEQ, LANE)
    sin_hi = jnp.where(half == 1, jnp.broadcast_to(sin, shape), 0.0).reshape(SEQ, LANE)
    return cos_t, sin_lo, sin_hi


def _hyena_filter_spectra(length, w1, b1, freq, w2, b2, w3):
    t = jnp.linspace(0.0, 1.0, length, dtype=F32)[:, None]
    w = 2.0 * math.pi * jnp.arange(length, dtype=F32)[:, None] / length
    bands = jnp.linspace(1e-4, HY_BANDS - 1, HY_BANDS, dtype=F32)[None]
    feats = jnp.concatenate([t, jnp.cos(bands * w), -jnp.sin(bands * w)], axis=-1)
    hdn = jnp.sin(freq[0] * (feats @ w1 + b1))
    hdn = jnp.sin(freq[1] * (hdn @ w2 + b2))
    h = (hdn @ w3).reshape(length, HY_ORDER, 2, HY_W)
    deltas = jnp.abs(jnp.linspace(HY_MIN_DECAY, HY_MAX_DECAY, HY_W, dtype=F32))
    h = h * jnp.exp(-t[:, :, None, None] * deltas)
    taps = jnp.concatenate([h[:, :, 0], jnp.zeros((1, HY_ORDER, HY_W), F32), h[:0:-1, :, 1]], axis=0)
    taps = taps / jnp.sum(jnp.abs(taps), axis=0, keepdims=True)
    return jnp.fft.rfft(taps, n=2 * length, axis=0)


def _dft_tables(n, period):
    idx = jnp.arange(n, dtype=jnp.int32)
    ang = ((idx[:, None] * idx[None, :]) % period).astype(F32) * (2.0 * math.pi / period)
    return jnp.cos(ang), jnp.sin(ang)


def _resident(shape, index_map):
    return pl.BlockSpec(shape, index_map, pipeline_mode=pl.Buffered(1))


HY_CB = 256
HY_ROWS = 512
HALO = 8


def _hyena_kernel(x1_ref, x2_ref, v_ref, cw1_ref, cw2_ref, cwv_ref, kr_ref, ki_ref, kn_ref, bias_ref,
                  c_ref, s_ref, o_ref, pad_ref, zb_ref, yr_ref, yi_ref, *, length, rows):
    starts = range(0, length, rows)
    zeros = jnp.zeros((HALO, HY_CB), F32)
    pad_ref[0:HALO, :] = zeros
    pad_ref[HALO + length:, :] = zeros

    def short_conv(w_ref, r0):
        w = w_ref[...]
        return (pad_ref[r0 + HALO - 1:r0 + HALO - 1 + rows, :] * w[0:1]
                + pad_ref[r0 + HALO:r0 + HALO + rows, :] * w[1:2]
                + pad_ref[r0 + HALO + 1:r0 + HALO + 1 + rows, :] * w[2:3])

    t = lax.broadcasted_iota(jnp.int32, (rows, 1), 0)
    alt = (1 - 2 * (t & 1)).astype(F32)

    pad_ref[HALO:HALO + length, :] = v_ref[...]
    for r0 in starts:
        z = short_conv(cwv_ref, r0)
        o_ref[r0:r0 + rows, :] = z
        zb_ref[r0:r0 + rows, :] = z.astype(BF16)

    for n, (gate_ref, gw_ref) in enumerate(((x1_ref, cw1_ref), (x2_ref, cw2_ref))):
        pad_ref[HALO:HALO + length, :] = gate_ref[...]
        zn = jnp.zeros((1, HY_CB), F32)
        for r0 in starts:
            zb = zb_ref[...]
            zr = jnp.dot(c_ref[r0:r0 + rows, :], zb, preferred_element_type=F32)
            zi = jnp.dot(s_ref[r0:r0 + rows, :], zb, preferred_element_type=F32)
            kr = kr_ref[n, r0:r0 + rows, :]
            ki = ki_ref[n, r0:r0 + rows, :]
            yr_ref[r0:r0 + rows, :] = (zr * kr + zi * ki).astype(BF16)
            yi_ref[r0:r0 + rows, :] = (zr * ki - zi * kr).astype(BF16)
            zn = zn + jnp.sum(o_ref[r0:r0 + rows, :] * alt, axis=0, keepdims=True)
        yn = zn * kn_ref[n:n + 1, :]
        for r0 in starts:
            y = (jnp.dot(c_ref[r0:r0 + rows, :], yr_ref[...], preferred_element_type=F32)
                 - jnp.dot(s_ref[r0:r0 + rows, :], yi_ref[...], preferred_element_type=F32)
                 + alt * yn)
            z = short_conv(gw_ref, r0) * (y + bias_ref[n:n + 1, :] * o_ref[r0:r0 + rows, :])
            o_ref[r0:r0 + rows, :] = z
            zb_ref[r0:r0 + rows, :] = z.astype(BF16)


def _hyena(p_all, conv_w, kf, bias, dft, length, row_blk0, nb):
    wt = jnp.full((length, 1, 1), 1.0 / length, F32).at[0].set(0.5 / length)
    kr = jnp.transpose(jnp.real(kf[:length]) * wt, (1, 0, 2))
    ki = jnp.transpose(jnp.imag(kf[:length]) * wt, (1, 0, 2))
    kn = jnp.real(kf[length]) * (0.5 / length)
    col0 = OFF_HY // HY_CB
    per = HY_W // HY_CB
    part = lambda k: pl.BlockSpec((length, HY_CB), lambda c, b: (row_blk0 + b, col0 + k * per + c))
    cw = lambda k: _resident((HY_SHORT, HY_CB), lambda c, b: (0, k * per + c))
    return pl.pallas_call(
        functools.partial(_hyena_kernel, length=length, rows=min(length, HY_ROWS)),
        grid=(per, nb),
        in_specs=[part(0), part(1), part(2), cw(0), cw(1), cw(2),
                  _resident((HY_ORDER, length, HY_CB), lambda c, b: (0, 0, c)),
                  _resident((HY_ORDER, length, HY_CB), lambda c, b: (0, 0, c)),
                  _resident((HY_ORDER, HY_CB), lambda c, b: (0, c)),
                  _resident((HY_ORDER, HY_CB), lambda c, b: (0, c)),
                  _resident((length, length), lambda c, b: (0, 0)),
                  _resident((length, length), lambda c, b: (0, 0))],
        out_specs=pl.BlockSpec((length, HY_CB), lambda c, b: (b, c)),
        out_shape=jax.ShapeDtypeStruct((nb * length, HY_W), F32),
        scratch_shapes=[pltpu.VMEM((length + 2 * HALO, HY_CB), F32),
                        pltpu.VMEM((length, HY_CB), BF16),
                        pltpu.VMEM((length, HY_CB), BF16),
                        pltpu.VMEM((length, HY_CB), BF16)],
        compiler_params=_params(2),
        name="hyena_%d" % length,
    )(p_all, p_all, p_all, conv_w, conv_w, conv_w, kr, ki, kn, bias, dft[0], dft[1])


FN_CB = 256
FN_GW = FN_W // FN_GROUPS


def _fnet_kernel(u_ref, cg_ref, sg_ref, cl_ref, sl_ref, o_ref, *, scale):
    u = u_ref[...].astype(BF16)
    uc = jnp.dot(u, cg_ref[...], preferred_element_type=F32).astype(BF16)
    us = jnp.dot(u, sg_ref[...], preferred_element_type=F32).astype(BF16)
    y = (jnp.dot(cl_ref[...], uc, preferred_element_type=F32)
         - jnp.dot(sl_ref[...], us, preferred_element_type=F32))
    o_ref[...] = y * scale


def _fnet(p_all, group_dft, seq_dft, length, row_blk0, nb):
    col0 = OFF_FN // FN_CB
    return pl.pallas_call(
        functools.partial(_fnet_kernel, scale=(length * FN_GW) ** -0.5),
        grid=(nb, FN_W // FN_CB),
        in_specs=[pl.BlockSpec((length, FN_CB), lambda b, c: (row_blk0 + b, col0 + c)),
                  _resident((FN_CB, FN_CB), lambda b, c: (0, 0)),
                  _resident((FN_CB, FN_CB), lambda b, c: (0, 0)),
                  _resident((length, length), lambda b, c: (0, 0)),
                  _resident((length, length), lambda b, c: (0, 0))],
        out_specs=pl.BlockSpec((length, FN_CB), lambda b, c: (b, c)),
        out_shape=jax.ShapeDtypeStruct((nb * length, FN_W), F32),
        compiler_params=_params(2),
        name="fnet_%d" % length,
    )(p_all, group_dft[0], group_dft[1], seq_dft[0], seq_dft[1])


def _fnet_group_tables():
    cg, sg = _dft_tables(FN_GW, FN_GW)
    eye = jnp.eye(FN_CB // FN_GW, dtype=F32)
    return jnp.kron(eye, cg).astype(BF16), jnp.kron(eye, sg).astype(BF16)


DN_ROWS = CTX_LEN + SEQ
DN_NCH = DN_ROWS // DN_CHUNK
DN_CTX_CH = CTX_LEN // DN_CHUNK
DN_LAT0 = CTX_LEN + 2 * HALO
DN_BASE_LOG2 = 3


def _softplus(x):
    return jnp.maximum(x, 0.0) + jnp.log1p(jnp.exp(-jnp.abs(x)))


def _deltanet_kernel(qc_ref, kc_ref, vc_ref, zc_ref, abc_ref, ql_ref, kl_ref, vl_ref, zl_ref, abl_ref,
                     cwq_ref, cwk_ref, cwv_ref, alog_ref, dtb_ref, nw_ref, yl_ref, yc_ref,
                     pad_ref, qn_ref, kn_ref, vn_ref, gate_ref, u_ref, wq_ref, ik_ref, gl_ref, o_ref):
    head = pl.program_id(1)
    cs = DN_CHUNK

    zeros = jnp.zeros((HALO, LANE), F32)
    pad_ref[0:HALO, :] = zeros
    pad_ref[HALO + CTX_LEN:DN_LAT0, :] = zeros
    pad_ref[DN_LAT0 + SEQ:, :] = zeros

    def conv_silu(c_ref, l_ref, w_ref):
        pad_ref[HALO:HALO + CTX_LEN, :] = c_ref[...]
        pad_ref[DN_LAT0:DN_LAT0 + SEQ, :] = l_ref[...]
        w = w_ref[...]
        parts = []
        for start, n in ((HALO, CTX_LEN), (DN_LAT0, SEQ)):
            y = (pad_ref[start - 1:start - 1 + n, :] * w[0:1] + pad_ref[start:start + n, :] * w[1:2]
                 + pad_ref[start + 1:start + 1 + n, :] * w[2:3])
            parts.append(y * jax.nn.sigmoid(y))
        return parts

    def l2n(x):
        return x * lax.rsqrt(jnp.sum(x * x, axis=-1, keepdims=True) + EPS)

    for dst_ref, (c_ref, l_ref, w_ref), norm, scale in (
            (qn_ref, (qc_ref, ql_ref, cwq_ref), True, DN_DK ** -0.5),
            (kn_ref, (kc_ref, kl_ref, cwk_ref), True, 1.0),
            (vn_ref, (vc_ref, vl_ref, cwv_ref), False, 1.0)):
        yc, yl = conv_silu(c_ref, l_ref, w_ref)
        if norm:
            yc, yl = l2n(yc) * scale, l2n(yl) * scale
        dst_ref[0:CTX_LEN, :] = yc
        dst_ref[CTX_LEN:, :] = yl

    lane = lax.broadcasted_iota(jnp.int32, (1, LANE), 1)
    for ab_ref, r0, n in ((abc_ref, 0, CTX_LEN), (abl_ref, CTX_LEN, SEQ)):
        ab = ab_ref[...]
        g_all = -jnp.exp(alog_ref[...]) * _softplus(ab + dtb_ref[...])
        b_all = jax.nn.sigmoid(ab)
        for d in range(2):
            for k, (src, base) in enumerate(((g_all, 0), (b_all, 2 * DN_HEADS))):
                col = jnp.sum(jnp.where(lane == base + d * DN_HEADS + head, src, 0.0), axis=-1,
                              keepdims=True)
                gate_ref[2 * d + k, r0:r0 + n, :] = jnp.broadcast_to(col, (n, LANE))

    ii = lax.broadcasted_iota(jnp.int32, (cs, LANE), 0)
    lane2 = lax.broadcasted_iota(jnp.int32, (cs, LANE), 1)
    jj = lane2 & (cs - 1)
    left = lane2 < cs
    eye = ii == jj
    incl = (ii >= jj, ii <= jj)
    strict = (ii > jj, ii < jj)
    ti = lax.broadcasted_iota(jnp.int32, (cs, cs), 0)
    tj = lax.broadcasted_iota(jnp.int32, (cs, cs), 1)
    tri = (jnp.where(ti >= tj, 1.0, 0.0).astype(BF16), jnp.where(ti <= tj, 1.0, 0.0).astype(BF16))
    eye_w = jnp.where(eye, 1.0, 0.0)
    same_blk = (ii >> DN_BASE_LOG2) == (jj >> DN_BASE_LOG2)
    merge_masks = tuple(
        jnp.logical_and((ii >> (s + 1)) == (jj >> (s + 1)), (ii >> s) != (jj >> s))
        for s in range(DN_BASE_LOG2, DN_CHUNK.bit_length() - 1))
    zero_blk = jnp.zeros((cs, LANE), BF16)
    nt = (((1,), (1,)), ((), ()))

    def lhs_dot(m_w, rhs):
        lhs = jnp.where(left, m_w, 0.0).astype(BF16)
        rhs16 = rhs.astype(BF16)
        return jnp.dot(lhs, jnp.concatenate([rhs16, jnp.zeros_like(rhs16)], axis=0),
                       preferred_element_type=F32)

    def chunk_body(c, carry):
        r0 = pl.multiple_of(c * cs, cs)
        rows = pl.ds(r0, cs)
        q = qn_ref[rows, :]
        k = kn_ref[rows, :]
        v = vn_ref[rows, :]
        q16 = q.astype(BF16)
        k16 = k.astype(BF16)
        k16x2 = jnp.concatenate([k16, k16], axis=0)
        kk = lax.dot_general(k16, k16x2, nt, preferred_element_type=F32)
        qk = lax.dot_general(q16, k16x2, nt, preferred_element_type=F32)
        for d in range(2):
            g = gate_ref[2 * d, rows, :]
            beta = gate_ref[2 * d + 1, rows, :]
            g_hi = g.astype(BF16)
            g_lo = (g - g_hi.astype(F32)).astype(BF16)
            gc = (jnp.dot(tri[d], g_hi, preferred_element_type=F32)
                  + jnp.dot(tri[d], g_lo, preferred_element_type=F32))
            last = cs - 1 if d == 0 else 0
            gc_last = gc[last:last + 1, :]
            gc_row = jnp.sum(jnp.where(eye, gc, 0.0), axis=0, keepdims=True)
            decay = jnp.where(incl[d], jnp.exp(jnp.where(incl[d], gc - gc_row, 0.0)), 0.0)
            a_mat = jnp.where(strict[d], kk * beta * decay, 0.0)
            t_w = eye_w
            p_w = jnp.where(same_blk, -a_mat, 0.0)
            for lvl in range(DN_BASE_LOG2):
                rhs = jnp.concatenate([t_w, p_w], axis=1) if lvl < DN_BASE_LOG2 - 1 else t_w
                out = lhs_dot(p_w, rhs)
                t_w = t_w + out[:, 0:LANE]
                if lvl < DN_BASE_LOG2 - 1:
                    p_w = out[:, LANE:]
            for e_mask in merge_masks:
                et = lhs_dot(jnp.where(e_mask, a_mat, 0.0), t_w)
                t_w = t_w - lhs_dot(t_w, et)
            e_gc = jnp.exp(gc)
            rhs = jnp.concatenate([v * beta, k * beta * e_gc], axis=1)
            uw = lhs_dot(t_w, rhs)
            intra = jnp.where(incl[d], qk * decay, 0.0)
            kd = k * jnp.exp(gc_last - gc)
            kd_t = jnp.concatenate([kd, jnp.zeros((LANE - cs, LANE), F32)], axis=0).T
            idx = d * DN_NCH + c
            u_ref[d, rows, :] = uw[:, 0:LANE]
            wq_ref[idx, 0:cs, :] = uw[:, LANE:].astype(BF16)
            wq_ref[idx, cs:, :] = (q * e_gc).astype(BF16)
            ik_ref[idx, 0:cs, :] = intra.astype(BF16)
            ik_ref[idx, cs:, :] = kd_t.astype(BF16)
            gl_ref[idx] = jnp.broadcast_to(jnp.exp(gc_last), (HALO, LANE))
        return carry

    lax.fori_loop(0, DN_NCH, chunk_body, 0)

    def scan_body(n, states):
        rev_c = jnp.where(n < DN_CTX_CH, DN_CTX_CH - 1 - n, DN_NCH + DN_CTX_CH - 1 - n)
        new_states = []
        for d, c in ((0, n), (1, rev_c)):
            s = states[d]
            idx = d * DN_NCH + c
            rows = pl.ds(pl.multiple_of(c * cs, cs), cs)
            ws_qs = jnp.dot(wq_ref[idx], s.astype(BF16), preferred_element_type=F32)
            v_new = u_ref[d, rows, :] - ws_qs[0:cs]
            v_pad = jnp.concatenate([v_new.astype(BF16), zero_blk], axis=0)
            r = jnp.dot(ik_ref[idx], v_pad, preferred_element_type=F32)
            o_ref[d, rows, :] = ws_qs[cs:] + r[0:cs]
            new_states.append(s * gl_ref[idx][0:1, :] + r[cs:])
        return tuple(new_states)

    s0 = jnp.zeros((DN_DK, DN_DV), F32)
    lax.fori_loop(0, DN_NCH, scan_body, (s0, s0))

    for y_ref, z_ref, r0, n in ((yc_ref, zc_ref, 0, CTX_LEN), (yl_ref, zl_ref, CTX_LEN, SEQ)):
        o = o_ref[0, r0:r0 + n, :] + o_ref[1, r0:r0 + n, :]
        z = z_ref[...]
        y = o * lax.rsqrt(jnp.mean(o * o, axis=-1, keepdims=True) + EPS) * nw_ref[...]
        y_ref[...] = y * (z * jax.nn.sigmoid(z))


def _deltanet(p_all, conv_w, a_log, dt_bias, norm_w, nb):
    ctx_blk0 = nb * SEQ // CTX_LEN
    qcol = OFF_DN // LANE
    zcol = OFF_DZ // LANE
    abcol = OFF_DA // LANE
    lat = lambda col: pl.BlockSpec((SEQ, LANE), lambda b, h: (b, col + h))
    ctx = lambda col: pl.BlockSpec((CTX_LEN, LANE), lambda b, h: (ctx_blk0 + b, col + h))
    cw = lambda k: pl.BlockSpec((3, LANE), lambda b, h: (0, k * DN_HEADS + h))
    small = pl.BlockSpec((1, LANE), lambda b, h: (0, 0))
    pad = lambda v: jnp.pad(v.reshape(1, -1), ((0, 0), (0, LANE - v.size)))
    n_idx = 2 * DN_NCH
    return pl.pallas_call(
        _deltanet_kernel,
        grid=(nb, DN_HEADS),
        in_specs=[ctx(qcol), ctx(qcol + DN_HEADS), ctx(qcol + 2 * DN_HEADS), ctx(zcol),
                  pl.BlockSpec((CTX_LEN, LANE), lambda b, h: (ctx_blk0 + b, abcol)),
                  lat(qcol), lat(qcol + DN_HEADS), lat(qcol + 2 * DN_HEADS), lat(zcol),
                  pl.BlockSpec((SEQ, LANE), lambda b, h: (b, abcol)),
                  cw(0), cw(1), cw(2), small, small, small],
        out_specs=[pl.BlockSpec((SEQ, LANE), lambda b, h: (b, h)),
                   pl.BlockSpec((CTX_LEN, LANE), lambda b, h: (b, h))],
        out_shape=[jax.ShapeDtypeStruct((nb * SEQ, BR_W), F32),
                   jax.ShapeDtypeStruct((nb * CTX_LEN, BR_W), F32)],
        scratch_shapes=[
            pltpu.VMEM((DN_LAT0 + SEQ + HALO, LANE), F32),
            pltpu.VMEM((DN_ROWS, LANE), F32),
            pltpu.VMEM((DN_ROWS, LANE), F32),
            pltpu.VMEM((DN_ROWS, LANE), F32),
            pltpu.VMEM((4, DN_ROWS, LANE), F32),
            pltpu.VMEM((2, DN_ROWS, LANE), F32),
            pltpu.VMEM((n_idx, 2 * DN_CHUNK, LANE), BF16),
            pltpu.VMEM((n_idx, DN_CHUNK + DN_DK, LANE), BF16),
            pltpu.VMEM((n_idx, HALO, LANE), F32),
            pltpu.VMEM((2, DN_ROWS, LANE), F32),
        ],
        compiler_params=_params(2),
        name="deltanet",
    )(*([p_all] * 10), conv_w, conv_w, conv_w, pad(a_log), pad(dt_bias), norm_w.reshape(1, LANE))


def kernel(x, c, ctx, c_ctx, w_ada, b_ada, norm1, norm2, w_in, attn_qk_gain, attn_lambda, attn_subln,
           hy_conv, hy_w1, hy_b1, hy_freq, hy_w2, hy_b2, hy_w3, hy_bias, dn_conv, dn_a_log,
           dn_dt_bias, dn_norm, w_gate, w_branch, w_o, w_mlp1, w_mlp2):
    x_all = jnp.concatenate([x.reshape(N_LAT, D_MODEL), ctx.reshape(N_CTX, D_MODEL)], axis=0)
    cond = jax.nn.silu(jnp.concatenate([c, c_ctx[None, :]], axis=0))
    rope_tabs = _rope_tables()
    dft_lat = tuple(t.astype(BF16) for t in _dft_tables(SEQ, 2 * SEQ))
    dft_ctx = tuple(t.astype(BF16) for t in _dft_tables(CTX_LEN, 2 * CTX_LEN))
    fn_lat = tuple(t.astype(BF16) for t in _dft_tables(SEQ, SEQ))
    fn_ctx = tuple(t.astype(BF16) for t in _dft_tables(CTX_LEN, CTX_LEN))
    fn_group = _fnet_group_tables()

    for li in range(DEPTH):
        need_ctx = li < DEPTH - 1
        lambda_init = 0.8 - 0.6 * math.exp(-0.3 * li)
        mod = (cond @ w_ada[li] + b_ada[li]).reshape(BATCH + 1, 6, D_MODEL)
        mod = jnp.pad(mod, ((0, 0), (0, MOD_ROWS - 6), (0, 0)))
        w_in_pad = jnp.pad(w_in[li], ((0, 0), (0, N_IN_PAD - N_IN))).astype(BF16)

        p_all = _proj_in(x_all, mod, norm1[li][None, :], w_in_pad)

        q_gain = jnp.tile(attn_qk_gain[li, 0], 2)[None, :]
        k_gain = jnp.tile(attn_qk_gain[li, 1], 2)[None, :]
        subln = attn_subln[li][None, :]
        ya = _attention(p_all, attn_lambda[li], q_gain, k_gain, subln, rope_tabs, lambda_init, True)

        hy_filt = (hy_w1[li], hy_b1[li], hy_freq[li], hy_w2[li], hy_b2[li], hy_w3[li])
        yb = _hyena(p_all, hy_conv[li], _hyena_filter_spectra(SEQ, *hy_filt), hy_bias[li], dft_lat,
                    SEQ, 0, BATCH)
        yf = _fnet(p_all, fn_group, fn_lat, SEQ, 0, BATCH)
        yd, yd_c = _deltanet(p_all, dn_conv[li], dn_a_log[li], dn_dt_bias[li], dn_norm[li], BATCH)
        merge_w = (w_gate[li].astype(BF16), w_branch[li].astype(BF16), w_o[li].astype(BF16))
        x_all = _merge(x_all, mod, norm1[li][None, :], [ya, yb, yf, yd], *merge_w, 0, N_LAT)
        n_rows = N_LAT
        if need_ctx:
            ya_c = _attention(p_all, attn_lambda[li], q_gain, k_gain, subln, None, lambda_init, False)
            yb_c = _hyena(p_all, hy_conv[li], _hyena_filter_spectra(CTX_LEN, *hy_filt), hy_bias[li],
                          dft_ctx, CTX_LEN, N_LAT // CTX_LEN, BATCH)
            yf_c = _fnet(p_all, fn_group, fn_ctx, CTX_LEN, N_LAT // CTX_LEN, BATCH)
            x_all = _merge(x_all, mod, norm1[li][None, :], [ya_c, yb_c, yf_c, yd_c],
                           *merge_w, N_LAT // ROW_TILE, N_CTX)
            n_rows = N_ALL
        x_all = _mlp(x_all, mod, norm2[li][None, :], w_mlp1[li].astype(BF16),
                     w_mlp2[li].astype(BF16), n_rows)
    return x_all[:N_LAT].reshape(BATCH, SEQ, D_MODEL)
```

```python
import functools
import math

import jax
import jax.numpy as jnp
from jax import lax
from jax.experimental import pallas as pl
from jax.experimental.pallas import tpu as pltpu

F32 = jnp.float32
BF16 = jnp.bfloat16

D_MODEL = 1024
BATCH = 16
SEQ = 2048
DEPTH = 4
GRID_W = 64
CTX_LEN = 256
N_BRANCH = 4
BR_W = 512
A_HEADS = 4
A_DQK = 64
A_DV = 2 * A_DQK
ROPE_BASE = 10000.0
HY_W = BR_W
HY_ORDER = 2
HY_SHORT = 3
HY_BANDS = 16
HY_HID = 64
HY_MIN_DECAY = math.log(1e-2) / 1.5
HY_MAX_DECAY = math.log(1e-2) / 0.3
FN_W = BR_W
FN_GROUPS = 4
DN_HEADS = 4
DN_DK = 128
DN_DV = 128
DN_CHUNK = 64
D_FF = 4 * D_MODEL
EPS = 1e-6

OFF_AQ = 0
OFF_AK = OFF_AQ + A_HEADS * 2 * A_DQK
OFF_AV = OFF_AK + A_HEADS * 2 * A_DQK
OFF_HY = OFF_AV + A_HEADS * A_DV
OFF_FN = OFF_HY + (HY_ORDER + 1) * HY_W
OFF_DN = OFF_FN + FN_W
OFF_DZ = OFF_DN + DN_HEADS * (2 * DN_DK + DN_DV)
OFF_DA = OFF_DZ + DN_HEADS * DN_DV
OFF_DB = OFF_DA + 2 * DN_HEADS
N_IN = OFF_DB + 2 * DN_HEADS

LANE = 128
N_IN_PAD = 45 * LANE
N_LAT = BATCH * SEQ
N_CTX = BATCH * CTX_LEN
N_ALL = N_LAT + N_CTX
MOD_ROWS = 8
VMEM_LIMIT = 56 * 1024 * 1024

ROW_TILE = 256
ATT_QT = 256


def _params(n_axes):
    return pltpu.CompilerParams(dimension_semantics=("arbitrary",) * n_axes,
                                vmem_limit_bytes=VMEM_LIMIT)


def _mod_row(i):
    return jnp.minimum(i // (SEQ // ROW_TILE), BATCH)


def _ada_norm(x, nw, shift, scale):
    ms = jnp.mean(x * x, axis=-1, keepdims=True)
    return (x * lax.rsqrt(ms + EPS) * nw) * (1.0 + scale) + shift


def _proj_in_kernel(x_ref, mod_ref, nw_ref, w_ref, p_ref):
    h = _ada_norm(x_ref[...], nw_ref[...], mod_ref[0, 0:1, :], mod_ref[0, 1:2, :])
    p_ref[...] = jnp.dot(h.astype(BF16), w_ref[...], preferred_element_type=F32)


def _proj_in(x_all, mod, nw, w_in_pad):
    n_tiles = N_ALL // ROW_TILE
    return pl.pallas_call(
        _proj_in_kernel,
        grid=(n_tiles,),
        in_specs=[
            pl.BlockSpec((ROW_TILE, D_MODEL), lambda i: (i, 0)),
            pl.BlockSpec((1, MOD_ROWS, D_MODEL), lambda i: (_mod_row(i), 0, 0)),
            pl.BlockSpec((1, D_MODEL), lambda i: (0, 0)),
            pl.BlockSpec((D_MODEL, N_IN_PAD), lambda i: (0, 0)),
        ],
        out_specs=pl.BlockSpec((ROW_TILE, N_IN_PAD), lambda i: (i, 0)),
        out_shape=jax.ShapeDtypeStruct((N_ALL, N_IN_PAD), F32),
        compiler_params=_params(1),
        name="proj_in",
    )(x_all, mod, nw, w_in_pad)


def _merge_kernel(x_ref, mod_ref, nw_ref, ya_ref, yb_ref, yf_ref, yd_ref, wg_ref, wb_ref, wo_ref,
                  o_ref):
    x = x_ref[...]
    h = _ada_norm(x, nw_ref[...], mod_ref[0, 0:1, :], mod_ref[0, 1:2, :]).astype(BF16)
    acc = None
    for n, y_ref in enumerate((ya_ref, yb_ref, yf_ref, yd_ref)):
        gate = jax.nn.sigmoid(jnp.dot(h, wg_ref[n], preferred_element_type=F32))
        br = jnp.dot(y_ref[...].astype(BF16), wb_ref[n], preferred_element_type=F32)
        acc = gate * br if acc is None else acc + gate * br
    out = jnp.dot(acc.astype(BF16), wo_ref[...], preferred_element_type=F32)
    o_ref[...] = x + mod_ref[0, 2:3, :] * out


def _merge(x_all, mod, nw, ys, wg, wb, wo, row_blk0, n_rows):
    xrow = lambda i: (row_blk0 + i, 0)
    yrow = lambda i: (i, 0)
    const2 = lambda i: (0, 0)
    const3 = lambda i: (0, 0, 0)
    return pl.pallas_call(
        _merge_kernel,
        grid=(n_rows // ROW_TILE,),
        in_specs=[
            pl.BlockSpec((ROW_TILE, D_MODEL), xrow),
            pl.BlockSpec((1, MOD_ROWS, D_MODEL), lambda i: (_mod_row(row_blk0 + i), 0, 0)),
            pl.BlockSpec((1, D_MODEL), const2),
        ] + [pl.BlockSpec((ROW_TILE, BR_W), yrow)] * N_BRANCH + [
            pl.BlockSpec((N_BRANCH, D_MODEL, D_MODEL), const3),
            pl.BlockSpec((N_BRANCH, BR_W, D_MODEL), const3),
            pl.BlockSpec((D_MODEL, D_MODEL), const2),
        ],
        out_specs=pl.BlockSpec((ROW_TILE, D_MODEL), xrow),
        out_shape=jax.ShapeDtypeStruct((N_ALL, D_MODEL), F32),
        input_output_aliases={0: 0},
        compiler_params=_params(1),
        name="merge",
    )(x_all, mod, nw, *ys, wg, wb, wo)


def _mlp_kernel(x_ref, mod_ref, nw_ref, w1_ref, w2_ref, o_ref):
    x = x_ref[...]
    h = _ada_norm(x, nw_ref[...], mod_ref[0, 3:4, :], mod_ref[0, 4:5, :]).astype(BF16)
    u = jnp.maximum(jnp.dot(h, w1_ref[...], preferred_element_type=F32), 0.0)
    out = jnp.dot((u * u).astype(BF16), w2_ref[...], preferred_element_type=F32)
    o_ref[...] = x + mod_ref[0, 5:6, :] * out


def _mlp(x_all, mod, nw, w1, w2, n_rows):
    row = lambda i: (i, 0)
    const2 = lambda i: (0, 0)
    return pl.pallas_call(
        _mlp_kernel,
        grid=(n_rows // ROW_TILE,),
        in_specs=[
            pl.BlockSpec((ROW_TILE, D_MODEL), row),
            pl.BlockSpec((1, MOD_ROWS, D_MODEL), lambda i: (_mod_row(i), 0, 0)),
            pl.BlockSpec((1, D_MODEL), const2),
            pl.BlockSpec((D_MODEL, D_FF), const2),
            pl.BlockSpec((D_FF, D_MODEL), const2),
        ],
        out_specs=pl.BlockSpec((ROW_TILE, D_MODEL), row),
        out_shape=jax.ShapeDtypeStruct((N_ALL, D_MODEL), F32),
        input_output_aliases={0: 0},
        compiler_params=_params(1),
        name="mlp",
    )(x_all, mod, nw, w1, w2)


def _qk_norm(x, gain):
    lane = lax.broadcasted_iota(jnp.int32, x.shape, 1)
    first = lane < A_DQK
    sq = x * x
    s_all = jnp.sum(sq, axis=-1, keepdims=True)
    s_first = jnp.sum(jnp.where(first, sq, 0.0), axis=-1, keepdims=True)
    ms = jnp.where(first, s_first, s_all - s_first) * (1.0 / A_DQK)
    return x * lax.rsqrt(ms + EPS) * gain


def _rope(x, cos, sin_lo, sin_hi):
    up = pltpu.roll(x, LANE - A_DQK // 4, 1)
    down = pltpu.roll(x, A_DQK // 4, 1)
    return x * cos + up * sin_lo + down * sin_hi


def _attn_kernel(lam_ref, qg_ref, kg_ref, sub_ref, q_ref, *rest, with_latent, lambda_init):
    if with_latent:
        (kc_ref, vc_ref, kl_ref, vl_ref, cq_ref, sq_lo_ref, sq_hi_ref,
         ck_ref, sk_lo_ref, sk_hi_ref, o_ref, kn_ref, vn_ref) = rest
    else:
        kc_ref, vc_ref, o_ref, kn_ref, vn_ref = rest

    @pl.when(pl.program_id(2) == 0)
    def _prep_keys():
        vn_ref[:, LANE:] = jnp.ones((vn_ref.shape[0], LANE), BF16)
        kn_ref[0:CTX_LEN, :] = _qk_norm(kc_ref[...], kg_ref[...]).astype(BF16)
        vn_ref[0:CTX_LEN, 0:LANE] = vc_ref[...].astype(BF16)
        if with_latent:
            kl = _qk_norm(kl_ref[...], kg_ref[...])
            kl = _rope(kl, ck_ref[...], sk_lo_ref[...], sk_hi_ref[...])
            kn_ref[CTX_LEN:, :] = kl.astype(BF16)
            vn_ref[CTX_LEN:, 0:LANE] = vl_ref[...].astype(BF16)

    q = _qk_norm(q_ref[...], qg_ref[...])
    if with_latent:
        q = _rope(q, cq_ref[...], sq_lo_ref[...], sq_hi_ref[...])
    q = q * (A_DQK ** -0.5 * math.log2(math.e))
    lane = lax.broadcasted_iota(jnp.int32, q.shape, 1)
    first = lane < A_DQK
    kn = kn_ref[...]
    vn = vn_ref[...]

    def component(qc):
        s = lax.dot_general(qc.astype(BF16), kn, (((1,), (1,)), ((), ())),
                            preferred_element_type=F32)
        m = jnp.max(s, axis=-1, keepdims=True)
        e = jnp.exp2((s - m).astype(BF16))
        ov = jnp.dot(e, vn, preferred_element_type=F32)
        return ov[:, 0:LANE] / ov[:, LANE:LANE + 1]

    lf = lam_ref[...]
    lam = (jnp.exp(jnp.sum(lf[0:1] * lf[1:2], axis=-1, keepdims=True))
           - jnp.exp(jnp.sum(lf[2:3] * lf[3:4], axis=-1, keepdims=True)) + lambda_init)
    o = component(jnp.where(first, q, 0.0)) - lam * component(jnp.where(first, 0.0, q))
    ms = jnp.mean(o * o, axis=-1, keepdims=True)
    o_ref[...] = (o * lax.rsqrt(ms + EPS) * sub_ref[...]) * (1.0 - lambda_init)


def _attention(p, lam_vec, q_gain, k_gain, subln, rope_tabs, lambda_init, with_latent, nb):
    kcol = OFF_AK // LANE
    vcol = OFF_AV // LANE
    ctx_blk0 = nb * SEQ // CTX_LEN
    small = lambda b, h, i: (0, 0)
    if with_latent:
        qt, nq = ATT_QT, SEQ // ATT_QT
        q_spec = pl.BlockSpec((qt, LANE), lambda b, h, i: (b * nq + i, h))
        n_keys = CTX_LEN + SEQ
    else:
        qt, nq = CTX_LEN, 1
        q_spec = pl.BlockSpec((qt, LANE), lambda b, h, i: (ctx_blk0 + b, h))
        n_keys = CTX_LEN
    in_specs = [
        pl.BlockSpec((4, A_DQK), small),
        pl.BlockSpec((1, LANE), small),
        pl.BlockSpec((1, LANE), small),
        pl.BlockSpec((1, LANE), small),
        q_spec,
        pl.BlockSpec((CTX_LEN, LANE), lambda b, h, i: (ctx_blk0 + b, kcol + h)),
        pl.BlockSpec((CTX_LEN, LANE), lambda b, h, i: (ctx_blk0 + b, vcol + h)),
    ]
    args = [lam_vec, q_gain, k_gain, subln, p, p, p]
    if with_latent:
        in_specs += [
            pl.BlockSpec((SEQ, LANE), lambda b, h, i: (b, kcol + h)),
            pl.BlockSpec((SEQ, LANE), lambda b, h, i: (b, vcol + h)),
        ] + [pl.BlockSpec((qt, LANE), lambda b, h, i: (i, 0))] * 3 + [
            pl.BlockSpec((SEQ, LANE), small)] * 3
        args += [p, p] + list(rope_tabs) + list(rope_tabs)
    n_rows = nb * (SEQ if with_latent else CTX_LEN)
    return pl.pallas_call(
        functools.partial(_attn_kernel, with_latent=with_latent, lambda_init=lambda_init),
        grid=(nb, A_HEADS, nq),
        in_specs=in_specs,
        out_specs=pl.BlockSpec((qt, LANE), lambda b, h, i: (b * nq + i, h)),
        out_shape=jax.ShapeDtypeStruct((n_rows, A_HEADS * A_DV), F32),
        scratch_shapes=[pltpu.VMEM((n_keys, LANE), BF16), pltpu.VMEM((n_keys, 2 * LANE), BF16)],
        compiler_params=_params(3),
        name="diff_attn_latent" if with_latent else "diff_attn_ctx",
    )(*args)


def _rope_tables():
    t = jnp.arange(SEQ, dtype=jnp.int32)
    pos = jnp.stack([(t // GRID_W).astype(F32), (t % GRID_W).astype(F32)], axis=1)
    n_freq = A_DQK // 4
    inv = ROPE_BASE ** (-jnp.arange(n_freq, dtype=F32) / n_freq)
    ang = pos[:, :, None] * inv
    cos = jnp.cos(ang)[:, None, :, None, :]
    sin = jnp.sin(ang)[:, None, :, None, :]
    shape = (SEQ, 2, 2, 2, n_freq)
    half = jnp.arange(2).reshape(1, 1, 1, 2, 1)
    cos_t = jnp.broadcast_to(cos, shape).reshape(SEQ, LANE)
    sin_lo = jnp.where(half == 0, -jnp.broadcast_to(sin, shape), 0.0).reshape(SEQ, LANE)
    sin_hi = jnp.where(half == 1, jnp.broadcast_to(sin, shape), 0.0).reshape(SEQ, LANE)
    return cos_t, sin_lo, sin_hi


def _hyena_filter_spectra(length, w1, b1, freq, w2, b2, w3):
    t = jnp.linspace(0.0, 1.0, length, dtype=F32)[:, None]
    w = 2.0 * math.pi * jnp.arange(length, dtype=F32)[:, None] / length
    bands = jnp.linspace(1e-4, HY_BANDS - 1, HY_BANDS, dtype=F32)[None]
    feats = jnp.concatenate([t, jnp.cos(bands * w), -jnp.sin(bands * w)], axis=-1)
    hdn = jnp.sin(freq[0] * (feats @ w1 + b1))
    hdn = jnp.sin(freq[1] * (hdn @ w2 + b2))
    h = (hdn @ w3).reshape(length, HY_ORDER, 2, HY_W)
    deltas = jnp.abs(jnp.linspace(HY_MIN_DECAY, HY_MAX_DECAY, HY_W, dtype=F32))
    h = h * jnp.exp(-t[:, :, None, None] * deltas)
    taps = jnp.concatenate([h[:, :, 0], jnp.zeros((1, HY_ORDER, HY_W), F32), h[:0:-1, :, 1]], axis=0)
    taps = taps / jnp.sum(jnp.abs(taps), axis=0, keepdims=True)
    return jnp.fft.rfft(taps, n=2 * length, axis=0)


def _dft_tables(n, period):
    idx = jnp.arange(n, dtype=jnp.int32)
    ang = ((idx[:, None] * idx[None, :]) % period).astype(F32) * (2.0 * math.pi / period)
    return jnp.cos(ang), jnp.sin(ang)


def _resident(shape, index_map):
    return pl.BlockSpec(shape, index_map, pipeline_mode=pl.Buffered(1))


HY_CB = 256
HY_ROWS = 512
HALO = 8


def _hyena_kernel(x1_ref, x2_ref, v_ref, cw1_ref, cw2_ref, cwv_ref, kr_ref, ki_ref, kn_ref, bias_ref,
                  c_ref, s_ref, o_ref, pad_ref, zb_ref, yr_ref, yi_ref, *, length, rows):
    starts = range(0, length, rows)
    zeros = jnp.zeros((HALO, HY_CB), F32)
    pad_ref[0:HALO, :] = zeros
    pad_ref[HALO + length:, :] = zeros

    def short_conv(w_ref, r0):
        w = w_ref[...]
        return (pad_ref[r0 + HALO - 1:r0 + HALO - 1 + rows, :] * w[0:1]
                + pad_ref[r0 + HALO:r0 + HALO + rows, :] * w[1:2]
                + pad_ref[r0 + HALO + 1:r0 + HALO + 1 + rows, :] * w[2:3])

    t = lax.broadcasted_iota(jnp.int32, (rows, 1), 0)
    alt = (1 - 2 * (t & 1)).astype(F32)

    pad_ref[HALO:HALO + length, :] = v_ref[...]
    for r0 in starts:
        z = short_conv(cwv_ref, r0)
        o_ref[r0:r0 + rows, :] = z
        zb_ref[r0:r0 + rows, :] = z.astype(BF16)

    for n, (gate_ref, gw_ref) in enumerate(((x1_ref, cw1_ref), (x2_ref, cw2_ref))):
        pad_ref[HALO:HALO + length, :] = gate_ref[...]
        zn = jnp.zeros((1, HY_CB), F32)
        for r0 in starts:
            zb = zb_ref[...]
            zr = jnp.dot(c_ref[r0:r0 + rows, :], zb, preferred_element_type=F32)
            zi = jnp.dot(s_ref[r0:r0 + rows, :], zb, preferred_element_type=F32)
            kr = kr_ref[n, r0:r0 + rows, :]
            ki = ki_ref[n, r0:r0 + rows, :]
            yr_ref[r0:r0 + rows, :] = (zr * kr + zi * ki).astype(BF16)
            yi_ref[r0:r0 + rows, :] = (zr * ki - zi * kr).astype(BF16)
            zn = zn + jnp.sum(o_ref[r0:r0 + rows, :] * alt, axis=0, keepdims=True)
        yn = zn * kn_ref[n:n + 1, :]
        for r0 in starts:
            y = (jnp.dot(c_ref[r0:r0 + rows, :], yr_ref[...], preferred_element_type=F32)
                 - jnp.dot(s_ref[r0:r0 + rows, :], yi_ref[...], preferred_element_type=F32)
                 + alt * yn)
            z = short_conv(gw_ref, r0) * (y + bias_ref[n:n + 1, :] * o_ref[r0:r0 + rows, :])
            o_ref[r0:r0 + rows, :] = z
            zb_ref[r0:r0 + rows, :] = z.astype(BF16)


def _hyena(p_all, conv_w, kf, bias, dft, length, row_blk0, nb):
    wt = jnp.full((length, 1, 1), 1.0 / length, F32).at[0].set(0.5 / length)
    kr = jnp.transpose(jnp.real(kf[:length]) * wt, (1, 0, 2))
    ki = jnp.transpose(jnp.imag(kf[:length]) * wt, (1, 0, 2))
    kn = jnp.real(kf[length]) * (0.5 / length)
    col0 = OFF_HY // HY_CB
    per = HY_W // HY_CB
    part = lambda k: pl.BlockSpec((length, HY_CB), lambda c, b: (row_blk0 + b, col0 + k * per + c))
    cw = lambda k: _resident((HY_SHORT, HY_CB), lambda c, b: (0, k * per + c))
    return pl.pallas_call(
        functools.partial(_hyena_kernel, length=length, rows=min(length, HY_ROWS)),
        grid=(per, nb),
        in_specs=[part(0), part(1), part(2), cw(0), cw(1), cw(2),
                  _resident((HY_ORDER, length, HY_CB), lambda c, b: (0, 0, c)),
                  _resident((HY_ORDER, length, HY_CB), lambda c, b: (0, 0, c)),
                  _resident((HY_ORDER, HY_CB), lambda c, b: (0, c)),
                  _resident((HY_ORDER, HY_CB), lambda c, b: (0, c)),
                  _resident((length, length), lambda c, b: (0, 0)),
                  _resident((length, length), lambda c, b: (0, 0))],
        out_specs=pl.BlockSpec((length, HY_CB), lambda c, b: (b, c)),
        out_shape=jax.ShapeDtypeStruct((nb * length, HY_W), F32),
        scratch_shapes=[pltpu.VMEM((length + 2 * HALO, HY_CB), F32),
                        pltpu.VMEM((length, HY_CB), BF16),
                        pltpu.VMEM((length, HY_CB), BF16),
                        pltpu.VMEM((length, HY_CB), BF16)],
        compiler_params=_params(2),
        name="hyena_%d" % length,
    )(p_all, p_all, p_all, conv_w, conv_w, conv_w, kr, ki, kn, bias, dft[0], dft[1])


FN_CB = 256
FN_GW = FN_W // FN_GROUPS


def _fnet_kernel(u_ref, cg_ref, sg_ref, cl_ref, sl_ref, o_ref, *, scale):
    u = u_ref[...].astype(BF16)
    uc = jnp.dot(u, cg_ref[...], preferred_element_type=F32).astype(BF16)
    us = jnp.dot(u, sg_ref[...], preferred_element_type=F32).astype(BF16)
    y = (jnp.dot(cl_ref[...], uc, preferred_element_type=F32)
         - jnp.dot(sl_ref[...], us, preferred_element_type=F32))
    o_ref[...] = y * scale


def _fnet(p_all, group_dft, seq_dft, length, row_blk0, nb):
    col0 = OFF_FN // FN_CB
    return pl.pallas_call(
        functools.partial(_fnet_kernel, scale=(length * FN_GW) ** -0.5),
        grid=(nb, FN_W // FN_CB),
        in_specs=[pl.BlockSpec((length, FN_CB), lambda b, c: (row_blk0 + b, col0 + c)),
                  _resident((FN_CB, FN_CB), lambda b, c: (0, 0)),
                  _resident((FN_CB, FN_CB), lambda b, c: (0, 0)),
                  _resident((length, length), lambda b, c: (0, 0)),
                  _resident((length, length), lambda b, c: (0, 0))],
        out_specs=pl.BlockSpec((length, FN_CB), lambda b, c: (b, c)),
        out_shape=jax.ShapeDtypeStruct((nb * length, FN_W), F32),
        compiler_params=_params(2),
        name="fnet_%d" % length,
    )(p_all, group_dft[0], group_dft[1], seq_dft[0], seq_dft[1])


def _fnet_group_tables():
    cg, sg = _dft_tables(FN_GW, FN_GW)
    eye = jnp.eye(FN_CB // FN_GW, dtype=F32)
    return jnp.kron(eye, cg).astype(BF16), jnp.kron(eye, sg).astype(BF16)


DN_ROWS = CTX_LEN + SEQ
DN_BLK = 256
DN_NCH = DN_ROWS // DN_BLK
DN_CTX_CH = CTX_LEN // DN_BLK
DN_UNROLL = 3
DN_LAT0 = CTX_LEN + 2 * HALO
DN_BASE_LOG2 = 3


def _softplus(x):
    return jnp.maximum(x, 0.0) + jnp.log1p(jnp.exp(-jnp.abs(x)))


def _deltanet_kernel(qc_ref, kc_ref, vc_ref, zc_ref, abc_ref, ql_ref, kl_ref, vl_ref, zl_ref, abl_ref,
                     cwq_ref, cwk_ref, cwv_ref, alog_ref, dtb_ref, nw_ref, yl_ref, yc_ref,
                     pad_ref, qn_ref, kn_ref, vn_ref, gate_ref, u_ref, wq_ref, ik_ref, gl_ref, o_ref):
    head = pl.program_id(1)
    cs = DN_BLK

    zeros = jnp.zeros((HALO, LANE), F32)
    pad_ref[0:HALO, :] = zeros
    pad_ref[HALO + CTX_LEN:DN_LAT0, :] = zeros
    pad_ref[DN_LAT0 + SEQ:, :] = zeros

    def conv_silu(c_ref, l_ref, w_ref):
        pad_ref[HALO:HALO + CTX_LEN, :] = c_ref[...]
        pad_ref[DN_LAT0:DN_LAT0 + SEQ, :] = l_ref[...]
        w = w_ref[...]
        parts = []
        for start, n in ((HALO, CTX_LEN), (DN_LAT0, SEQ)):
            y = (pad_ref[start - 1:start - 1 + n, :] * w[0:1] + pad_ref[start:start + n, :] * w[1:2]
                 + pad_ref[start + 1:start + 1 + n, :] * w[2:3])
            parts.append(y * jax.nn.sigmoid(y))
        return parts

    def l2n(x):
        return x * lax.rsqrt(jnp.sum(x * x, axis=-1, keepdims=True) + EPS)

    for dst_ref, (c_ref, l_ref, w_ref), norm, scale in (
            (qn_ref, (qc_ref, ql_ref, cwq_ref), True, DN_DK ** -0.5),
            (kn_ref, (kc_ref, kl_ref, cwk_ref), True, 1.0),
            (vn_ref, (vc_ref, vl_ref, cwv_ref), False, 1.0)):
        yc, yl = conv_silu(c_ref, l_ref, w_ref)
        if norm:
            yc, yl = l2n(yc) * scale, l2n(yl) * scale
        dst_ref[0:CTX_LEN, :] = yc
        dst_ref[CTX_LEN:, :] = yl

    lane = lax.broadcasted_iota(jnp.int32, (1, LANE), 1)
    for ab_ref, r0, n in ((abc_ref, 0, CTX_LEN), (abl_ref, CTX_LEN, SEQ)):
        ab = ab_ref[...]
        g_all = -jnp.exp(alog_ref[...]) * _softplus(ab + dtb_ref[...])
        b_all = jax.nn.sigmoid(ab)
        for d in range(2):
            for k, (src, base) in enumerate(((g_all, 0), (b_all, 2 * DN_HEADS))):
                col = jnp.sum(jnp.where(lane == base + d * DN_HEADS + head, src, 0.0), axis=-1,
                              keepdims=True)
                gate_ref[2 * d + k, r0:r0 + n, :] = jnp.broadcast_to(col, (n, LANE))

    nt = (((1,), (1,)), ((), ()))

    def mm(x, y):
        return jnp.dot(x.astype(BF16), y.astype(BF16), preferred_element_type=F32)

    def block_body(it, carry):
        ii = lax.broadcasted_iota(jnp.int32, (cs, cs), 0)
        jj = lax.broadcasted_iota(jnp.int32, (cs, cs), 1)
        eye = ii == jj
        incl = (ii >= jj, ii <= jj)
        tri = tuple(jnp.where(m, 1.0, 0.0).astype(BF16) for m in incl)
        blocks = [it * DN_UNROLL + j for j in range(DN_UNROLL)]
        rows = [pl.ds(pl.multiple_of(c * cs, cs), cs) for c in blocks]
        q = [qn_ref[r, :] for r in rows]
        k = [kn_ref[r, :] for r in rows]
        v = [vn_ref[r, :] for r in rows]
        k16 = [x.astype(BF16) for x in k]
        kk = [lax.dot_general(x, x, nt, preferred_element_type=F32) for x in k16]
        qk = [lax.dot_general(x.astype(BF16), y, nt, preferred_element_type=F32) for x, y in zip(q, k16)]
        chains = [(j, d) for j in range(DN_UNROLL) for d in range(2)]
        g = [gate_ref[2 * d, rows[j], :] for j, d in chains]
        beta = [gate_ref[2 * d + 1, rows[j], :] for j, d in chains]
        g_hi = [x.astype(BF16) for x in g]
        g_lo = [(x - h.astype(F32)).astype(BF16) for x, h in zip(g, g_hi)]
        gc = [jnp.dot(tri[d], h, preferred_element_type=F32) + jnp.dot(tri[d], l, preferred_element_type=F32)
              for (j, d), h, l in zip(chains, g_hi, g_lo)]
        gc_last = [x[cs - 1:cs, :] if d == 0 else x[0:1, :] for (j, d), x in zip(chains, gc)]
        decay, a_mat = [], []
        for (j, d), x, bt in zip(chains, gc, beta):
            gc2 = jnp.concatenate([x] * (cs // LANE), axis=1)
            gc_row = jnp.sum(jnp.where(eye, gc2, 0.0), axis=0, keepdims=True)
            dec = jnp.where(incl[d], jnp.exp(jnp.where(incl[d], gc2 - gc_row, 0.0)), 0.0)
            decay.append(dec)
            a_mat.append(jnp.where(eye, 0.0, kk[j] * jnp.concatenate([bt] * (cs // LANE), axis=1) * dec))
        same = (ii >> DN_BASE_LOG2) == (jj >> DN_BASE_LOG2)
        t_mat = [jnp.where(eye, 1.0, 0.0) for _ in chains]
        p_mat = [jnp.where(same, -a, 0.0) for a in a_mat]
        for lvl in range(DN_BASE_LOG2 - 1):
            out = [mm(p, jnp.concatenate([t, p], axis=1)) for t, p in zip(t_mat, p_mat)]
            t_mat = [t + o[:, 0:cs] for t, o in zip(t_mat, out)]
            p_mat = [o[:, cs:] for o in out]
        t_mat = [t + mm(p, t) for t, p in zip(t_mat, p_mat)]
        for s in range(DN_BASE_LOG2, cs.bit_length() - 1):
            e_mask = jnp.logical_and((ii >> (s + 1)) == (jj >> (s + 1)), (ii >> s) != (jj >> s))
            et = [mm(jnp.where(e_mask, a, 0.0), t) for a, t in zip(a_mat, t_mat)]
            t_mat = [t - mm(t, e) for t, e in zip(t_mat, et)]
        e_gc = [jnp.exp(x) for x in gc]
        uw = [mm(t, jnp.concatenate([v[j] * bt, k[j] * bt * e], axis=1))
              for (j, d), t, bt, e in zip(chains, t_mat, beta, e_gc)]
        for n, (j, d) in enumerate(chains):
            idx = d * DN_NCH + blocks[j]
            u_ref[d, rows[j], :] = uw[n][:, 0:LANE]
            wq_ref[idx, 0:cs, :] = uw[n][:, LANE:].astype(BF16)
            wq_ref[idx, cs:, :] = (q[j] * e_gc[n]).astype(BF16)
            ik_ref[idx, 0:cs, :] = (qk[j] * decay[n]).astype(BF16)
            ik_ref[idx, cs:, :] = (k[j] * jnp.exp(gc_last[n] - gc[n])).T.astype(BF16)
            gl_ref[idx] = jnp.broadcast_to(jnp.exp(gc_last[n]), (HALO, LANE))
        return carry

    lax.fori_loop(0, DN_NCH // DN_UNROLL, block_body, 0)

    def scan_body(n, states):
        rev_c = jnp.where(n < DN_CTX_CH, DN_CTX_CH - 1 - n, DN_NCH + DN_CTX_CH - 1 - n)
        blk = (n, rev_c)
        idx = [d * DN_NCH + blk[d] for d in range(2)]
        rows = [pl.ds(pl.multiple_of(blk[d] * cs, cs), cs) for d in range(2)]
        ws_qs = [jnp.dot(wq_ref[idx[d]], states[d].astype(BF16), preferred_element_type=F32)
                 for d in range(2)]
        v_new = [u_ref[d, rows[d], :] - ws_qs[d][0:cs] for d in range(2)]
        r = [jnp.dot(ik_ref[idx[d]], v_new[d].astype(BF16), preferred_element_type=F32) for d in range(2)]
        for d in range(2):
            o_ref[d, rows[d], :] = ws_qs[d][cs:] + r[d][0:cs]
        return tuple(states[d] * gl_ref[idx[d]][0:1, :] + r[d][cs:] for d in range(2))

    s0 = jnp.zeros((DN_DK, DN_DV), F32)
    lax.fori_loop(0, DN_NCH, scan_body, (s0, s0))

    for y_ref, z_ref, r0, n in ((yc_ref, zc_ref, 0, CTX_LEN), (yl_ref, zl_ref, CTX_LEN, SEQ)):
        o = o_ref[0, r0:r0 + n, :] + o_ref[1, r0:r0 + n, :]
        z = z_ref[...]
        y = o * lax.rsqrt(jnp.mean(o * o, axis=-1, keepdims=True) + EPS) * nw_ref[...]
        y_ref[...] = y * (z * jax.nn.sigmoid(z))


def _deltanet(p_all, conv_w, a_log, dt_bias, norm_w, nb):
    ctx_blk0 = nb * SEQ // CTX_LEN
    qcol = OFF_DN // LANE
    zcol = OFF_DZ // LANE
    abcol = OFF_DA // LANE
    lat = lambda col: pl.BlockSpec((SEQ, LANE), lambda b, h: (b, col + h))
    ctx = lambda col: pl.BlockSpec((CTX_LEN, LANE), lambda b, h: (ctx_blk0 + b, col + h))
    cw = lambda k: pl.BlockSpec((3, LANE), lambda b, h: (0, k * DN_HEADS + h))
    small = pl.BlockSpec((1, LANE), lambda b, h: (0, 0))
    pad = lambda v: jnp.pad(v.reshape(1, -1), ((0, 0), (0, LANE - v.size)))
    n_idx = 2 * DN_NCH
    return pl.pallas_call(
        _deltanet_kernel,
        grid=(nb, DN_HEADS),
        in_specs=[ctx(qcol), ctx(qcol + DN_HEADS), ctx(qcol + 2 * DN_HEADS), ctx(zcol),
                  pl.BlockSpec((CTX_LEN, LANE), lambda b, h: (ctx_blk0 + b, abcol)),
                  lat(qcol), lat(qcol + DN_HEADS), lat(qcol + 2 * DN_HEADS), lat(zcol),
                  pl.BlockSpec((SEQ, LANE), lambda b, h: (b, abcol)),
                  cw(0), cw(1), cw(2), small, small, small],
        out_specs=[pl.BlockSpec((SEQ, LANE), lambda b, h: (b, h)),
                   pl.BlockSpec((CTX_LEN, LANE), lambda b, h: (b, h))],
        out_shape=[jax.ShapeDtypeStruct((nb * SEQ, BR_W), F32),
                   jax.ShapeDtypeStruct((nb * CTX_LEN, BR_W), F32)],
        scratch_shapes=[
            pltpu.VMEM((DN_LAT0 + SEQ + HALO, LANE), F32),
            pltpu.VMEM((DN_ROWS, LANE), F32),
            pltpu.VMEM((DN_ROWS, LANE), F32),
            pltpu.VMEM((DN_ROWS, LANE), F32),
            pltpu.VMEM((4, DN_ROWS, LANE), F32),
            pltpu.VMEM((2, DN_ROWS, LANE), F32),
            pltpu.VMEM((n_idx, 2 * DN_BLK, LANE), BF16),
            pltpu.VMEM((n_idx, DN_BLK + DN_DK, DN_BLK), BF16),
            pltpu.VMEM((n_idx, HALO, LANE), F32),
            pltpu.VMEM((2, DN_ROWS, LANE), F32),
        ],
        compiler_params=_params(2),
        name="deltanet",
    )(*([p_all] * 10), conv_w, conv_w, conv_w, pad(a_log), pad(dt_bias), norm_w.reshape(1, LANE))


def kernel(x, c, ctx, c_ctx, w_ada, b_ada, norm1, norm2, w_in, attn_qk_gain, attn_lambda, attn_subln,
           hy_conv, hy_w1, hy_b1, hy_freq, hy_w2, hy_b2, hy_w3, hy_bias, dn_conv, dn_a_log,
           dn_dt_bias, dn_norm, w_gate, w_branch, w_o, w_mlp1, w_mlp2):
    x_all = jnp.concatenate([x.reshape(N_LAT, D_MODEL), ctx.reshape(N_CTX, D_MODEL)], axis=0)
    cond = jax.nn.silu(jnp.concatenate([c, c_ctx[None, :]], axis=0))
    rope_tabs = _rope_tables()
    dft_lat = tuple(t.astype(BF16) for t in _dft_tables(SEQ, 2 * SEQ))
    dft_ctx = tuple(t.astype(BF16) for t in _dft_tables(CTX_LEN, 2 * CTX_LEN))
    fn_lat = tuple(t.astype(BF16) for t in _dft_tables(SEQ, SEQ))
    fn_ctx = tuple(t.astype(BF16) for t in _dft_tables(CTX_LEN, CTX_LEN))
    fn_group = _fnet_group_tables()

    for li in range(DEPTH):
        need_ctx = li < DEPTH - 1
        lambda_init = 0.8 - 0.6 * math.exp(-0.3 * li)
        mod = (cond @ w_ada[li] + b_ada[li]).reshape(BATCH + 1, 6, D_MODEL)
        mod = jnp.pad(mod, ((0, 0), (0, MOD_ROWS - 6), (0, 0)))
        w_in_pad = jnp.pad(w_in[li], ((0, 0), (0, N_IN_PAD - N_IN))).astype(BF16)

        p_all = _proj_in(x_all, mod, norm1[li][None, :], w_in_pad)

        q_gain = jnp.tile(attn_qk_gain[li, 0], 2)[None, :]
        k_gain = jnp.tile(attn_qk_gain[li, 1], 2)[None, :]
        subln = attn_subln[li][None, :]
        ya = _attention(p_all, attn_lambda[li], q_gain, k_gain, subln, rope_tabs, lambda_init, True, BATCH)

        hy_filt = (hy_w1[li], hy_b1[li], hy_freq[li], hy_w2[li], hy_b2[li], hy_w3[li])
        yb = _hyena(p_all, hy_conv[li], _hyena_filter_spectra(SEQ, *hy_filt), hy_bias[li], dft_lat,
                    SEQ, 0, BATCH)
        yf = _fnet(p_all, fn_group, fn_lat, SEQ, 0, BATCH)
        yd, yd_c = _deltanet(p_all, dn_conv[li], dn_a_log[li], dn_dt_bias[li], dn_norm[li], BATCH)
        merge_w = (w_gate[li].astype(BF16), w_branch[li].astype(BF16), w_o[li].astype(BF16))
        x_all = _merge(x_all, mod, norm1[li][None, :], [ya, yb, yf, yd], *merge_w, 0, N_LAT)
        n_rows = N_LAT
        if need_ctx:
            ya_c = _attention(p_all, attn_lambda[li], q_gain, k_gain, subln, None, lambda_init, False,
                              BATCH)
            yb_c = _hyena(p_all, hy_conv[li], _hyena_filter_spectra(CTX_LEN, *hy_filt), hy_bias[li],
                          dft_ctx, CTX_LEN, N_LAT // CTX_LEN, BATCH)
            yf_c = _fnet(p_all, fn_group, fn_ctx, CTX_LEN, N_LAT // CTX_LEN, BATCH)
            x_all = _merge(x_all, mod, norm1[li][None, :], [ya_c, yb_c, yf_c, yd_c],
                           *merge_w, N_LAT // ROW_TILE, N_CTX)
            n_rows = N_ALL
        x_all = _mlp(x_all, mod, norm2[li][None, :], w_mlp1[li].astype(BF16),
                     w_mlp2[li].astype(BF16), n_rows)
    return x_all[:N_LAT].reshape(BATCH, SEQ, D_MODEL)
```

```python
import functools
import math

import jax
import jax.numpy as jnp
from jax import lax
from jax.experimental import pallas as pl
from jax.experimental.pallas import tpu as pltpu

F32 = jnp.float32
BF16 = jnp.bfloat16

D_MODEL = 1024
BATCH = 16
SEQ = 2048
DEPTH = 4
GRID_W = 64
CTX_LEN = 256
N_BRANCH = 4
BR_W = 512
A_HEADS = 4
A_DQK = 64
A_DV = 2 * A_DQK
ROPE_BASE = 10000.0
HY_W = BR_W
HY_ORDER = 2
HY_SHORT = 3
HY_BANDS = 16
HY_HID = 64
HY_MIN_DECAY = math.log(1e-2) / 1.5
HY_MAX_DECAY = math.log(1e-2) / 0.3
FN_W = BR_W
FN_GROUPS = 4
DN_HEADS = 4
DN_DK = 128
DN_DV = 128
DN_CHUNK = 64
D_FF = 4 * D_MODEL
EPS = 1e-6

OFF_AQ = 0
OFF_AK = OFF_AQ + A_HEADS * 2 * A_DQK
OFF_AV = OFF_AK + A_HEADS * 2 * A_DQK
OFF_HY = OFF_AV + A_HEADS * A_DV
OFF_FN = OFF_HY + (HY_ORDER + 1) * HY_W
OFF_DN = OFF_FN + FN_W
OFF_DZ = OFF_DN + DN_HEADS * (2 * DN_DK + DN_DV)
OFF_DA = OFF_DZ + DN_HEADS * DN_DV
OFF_DB = OFF_DA + 2 * DN_HEADS
N_IN = OFF_DB + 2 * DN_HEADS

LANE = 128
N_IN_PAD = 45 * LANE
N_LAT = BATCH * SEQ
N_CTX = BATCH * CTX_LEN
N_ALL = N_LAT + N_CTX
MOD_ROWS = 8
VMEM_LIMIT = 56 * 1024 * 1024

ROW_TILE = 256
ATT_QT = 512
ATT_ROWS = 256


def _params(n_axes):
    return pltpu.CompilerParams(dimension_semantics=("arbitrary",) * n_axes,
                                vmem_limit_bytes=VMEM_LIMIT)


def _mod_row(i):
    return jnp.minimum(i // (SEQ // ROW_TILE), BATCH)


def _ada_norm(x, nw, shift, scale):
    ms = jnp.mean(x * x, axis=-1, keepdims=True)
    return (x * lax.rsqrt(ms + EPS) * nw) * (1.0 + scale) + shift


def _proj_in_kernel(x_ref, mod_ref, nw_ref, w_ref, p_ref):
    h = _ada_norm(x_ref[...], nw_ref[...], mod_ref[0, 0:1, :], mod_ref[0, 1:2, :])
    p_ref[...] = jnp.dot(h.astype(BF16), w_ref[...], preferred_element_type=F32)


def _proj_in(x_all, mod, nw, w_in_pad):
    n_tiles = N_ALL // ROW_TILE
    return pl.pallas_call(
        _proj_in_kernel,
        grid=(n_tiles,),
        in_specs=[
            pl.BlockSpec((ROW_TILE, D_MODEL), lambda i: (i, 0)),
            pl.BlockSpec((1, MOD_ROWS, D_MODEL), lambda i: (_mod_row(i), 0, 0)),
            pl.BlockSpec((1, D_MODEL), lambda i: (0, 0)),
            pl.BlockSpec((D_MODEL, N_IN_PAD), lambda i: (0, 0)),
        ],
        out_specs=pl.BlockSpec((ROW_TILE, N_IN_PAD), lambda i: (i, 0)),
        out_shape=jax.ShapeDtypeStruct((N_ALL, N_IN_PAD), F32),
        compiler_params=_params(1),
        name="proj_in",
    )(x_all, mod, nw, w_in_pad)


def _merge_kernel(x_ref, mod_ref, nw_ref, ya_ref, yb_ref, yf_ref, yd_ref, wg_ref, wb_ref, wo_ref,
                  o_ref):
    x = x_ref[...]
    h = _ada_norm(x, nw_ref[...], mod_ref[0, 0:1, :], mod_ref[0, 1:2, :]).astype(BF16)
    acc = None
    for n, y_ref in enumerate((ya_ref, yb_ref, yf_ref, yd_ref)):
        gate = jax.nn.sigmoid(jnp.dot(h, wg_ref[n], preferred_element_type=F32))
        br = jnp.dot(y_ref[...].astype(BF16), wb_ref[n], preferred_element_type=F32)
        acc = gate * br if acc is None else acc + gate * br
    out = jnp.dot(acc.astype(BF16), wo_ref[...], preferred_element_type=F32)
    o_ref[...] = x + mod_ref[0, 2:3, :] * out


def _merge(x_all, mod, nw, ys, wg, wb, wo, row_blk0, n_rows):
    xrow = lambda i: (row_blk0 + i, 0)
    yrow = lambda i: (i, 0)
    const2 = lambda i: (0, 0)
    const3 = lambda i: (0, 0, 0)
    return pl.pallas_call(
        _merge_kernel,
        grid=(n_rows // ROW_TILE,),
        in_specs=[
            pl.BlockSpec((ROW_TILE, D_MODEL), xrow),
            pl.BlockSpec((1, MOD_ROWS, D_MODEL), lambda i: (_mod_row(row_blk0 + i), 0, 0)),
            pl.BlockSpec((1, D_MODEL), const2),
        ] + [pl.BlockSpec((ROW_TILE, BR_W), yrow)] * N_BRANCH + [
            pl.BlockSpec((N_BRANCH, D_MODEL, D_MODEL), const3),
            pl.BlockSpec((N_BRANCH, BR_W, D_MODEL), const3),
            pl.BlockSpec((D_MODEL, D_MODEL), const2),
        ],
        out_specs=pl.BlockSpec((ROW_TILE, D_MODEL), xrow),
        out_shape=jax.ShapeDtypeStruct((N_ALL, D_MODEL), F32),
        input_output_aliases={0: 0},
        compiler_params=_params(1),
        name="merge",
    )(x_all, mod, nw, *ys, wg, wb, wo)


def _mlp_kernel(x_ref, mod_ref, nw_ref, w1_ref, w2_ref, o_ref):
    x = x_ref[...]
    h = _ada_norm(x, nw_ref[...], mod_ref[0, 3:4, :], mod_ref[0, 4:5, :]).astype(BF16)
    u = jnp.maximum(jnp.dot(h, w1_ref[...], preferred_element_type=F32), 0.0)
    out = jnp.dot((u * u).astype(BF16), w2_ref[...], preferred_element_type=F32)
    o_ref[...] = x + mod_ref[0, 5:6, :] * out


def _mlp(x_all, mod, nw, w1, w2, n_rows):
    row = lambda i: (i, 0)
    const2 = lambda i: (0, 0)
    return pl.pallas_call(
        _mlp_kernel,
        grid=(n_rows // ROW_TILE,),
        in_specs=[
            pl.BlockSpec((ROW_TILE, D_MODEL), row),
            pl.BlockSpec((1, MOD_ROWS, D_MODEL), lambda i: (_mod_row(i), 0, 0)),
            pl.BlockSpec((1, D_MODEL), const2),
            pl.BlockSpec((D_MODEL, D_FF), const2),
            pl.BlockSpec((D_FF, D_MODEL), const2),
        ],
        out_specs=pl.BlockSpec((ROW_TILE, D_MODEL), row),
        out_shape=jax.ShapeDtypeStruct((N_ALL, D_MODEL), F32),
        input_output_aliases={0: 0},
        compiler_params=_params(1),
        name="mlp",
    )(x_all, mod, nw, w1, w2)


def _qk_norm(x, gain):
    lane = lax.broadcasted_iota(jnp.int32, x.shape, 1)
    first = lane < A_DQK
    sq = x * x
    s_all = jnp.sum(sq, axis=-1, keepdims=True)
    s_first = jnp.sum(jnp.where(first, sq, 0.0), axis=-1, keepdims=True)
    ms = jnp.where(first, s_first, s_all - s_first) * (1.0 / A_DQK)
    return x * lax.rsqrt(ms + EPS) * gain


def _rope(x, cos, sin_lo, sin_hi):
    up = pltpu.roll(x, LANE - A_DQK // 4, 1)
    down = pltpu.roll(x, A_DQK // 4, 1)
    return x * cos + up * sin_lo + down * sin_hi


def _attn_kernel(lam_ref, qg_ref, kg_ref, sub_ref, q_ref, *rest, with_latent, lambda_init):
    if with_latent:
        (kc_ref, vc_ref, kl_ref, vl_ref, cq_ref, sq_lo_ref, sq_hi_ref,
         ck_ref, sk_lo_ref, sk_hi_ref, o_ref, kn_ref, vn_ref) = rest
    else:
        kc_ref, vc_ref, o_ref, kn_ref, vn_ref = rest

    @pl.when(pl.program_id(2) == 0)
    def _prep_keys():
        vn_ref[:, LANE:] = jnp.ones((vn_ref.shape[0], LANE), BF16)
        kn_ref[0:CTX_LEN, :] = _qk_norm(kc_ref[...], kg_ref[...]).astype(BF16)
        vn_ref[0:CTX_LEN, 0:LANE] = vc_ref[...].astype(BF16)
        if with_latent:
            kl = _qk_norm(kl_ref[...], kg_ref[...])
            kl = _rope(kl, ck_ref[...], sk_lo_ref[...], sk_hi_ref[...])
            kn_ref[CTX_LEN:, :] = kl.astype(BF16)
            vn_ref[CTX_LEN:, 0:LANE] = vl_ref[...].astype(BF16)

    q = _qk_norm(q_ref[...], qg_ref[...])
    if with_latent:
        q = _rope(q, cq_ref[...], sq_lo_ref[...], sq_hi_ref[...])
    q = q * (A_DQK ** -0.5 * math.log2(math.e))
    lane = lax.broadcasted_iota(jnp.int32, q.shape, 1)
    first = lane < A_DQK
    kn = kn_ref[...]
    vn = vn_ref[...]

    n_grp = max(q.shape[0] // ATT_ROWS, 1)
    rows = q.shape[0] // n_grp
    qs = [jnp.where(first if c == 0 else jnp.logical_not(first), q, 0.0)[g * rows:(g + 1) * rows]
          for g in range(n_grp) for c in range(2)]
    s = [lax.dot_general(x.astype(BF16), kn, (((1,), (1,)), ((), ())), preferred_element_type=F32)
         for x in qs]
    e = [jnp.exp2((x - jnp.max(x, axis=-1, keepdims=True)).astype(BF16)) for x in s]
    ov = [jnp.dot(x, vn, preferred_element_type=F32) for x in e]
    oc = [x[:, 0:LANE] / x[:, LANE:LANE + 1] for x in ov]

    lf = lam_ref[...]
    lam = (jnp.exp(jnp.sum(lf[0:1] * lf[1:2], axis=-1, keepdims=True))
           - jnp.exp(jnp.sum(lf[2:3] * lf[3:4], axis=-1, keepdims=True)) + lambda_init)
    o = jnp.concatenate([oc[2 * g] - lam * oc[2 * g + 1] for g in range(n_grp)], axis=0)
    ms = jnp.mean(o * o, axis=-1, keepdims=True)
    o_ref[...] = (o * lax.rsqrt(ms + EPS) * sub_ref[...]) * (1.0 - lambda_init)


def _attention(p, lam_vec, q_gain, k_gain, subln, rope_tabs, lambda_init, with_latent, nb):
    kcol = OFF_AK // LANE
    vcol = OFF_AV // LANE
    ctx_blk0 = nb * SEQ // CTX_LEN
    small = lambda b, h, i: (0, 0)
    if with_latent:
        qt, nq = ATT_QT, SEQ // ATT_QT
        q_spec = pl.BlockSpec((qt, LANE), lambda b, h, i: (b * nq + i, h))
        n_keys = CTX_LEN + SEQ
    else:
        qt, nq = CTX_LEN, 1
        q_spec = pl.BlockSpec((qt, LANE), lambda b, h, i: (ctx_blk0 + b, h))
        n_keys = CTX_LEN
    in_specs = [
        pl.BlockSpec((4, A_DQK), small),
        pl.BlockSpec((1, LANE), small),
        pl.BlockSpec((1, LANE), small),
        pl.BlockSpec((1, LANE), small),
        q_spec,
        pl.BlockSpec((CTX_LEN, LANE), lambda b, h, i: (ctx_blk0 + b, kcol + h)),
        pl.BlockSpec((CTX_LEN, LANE), lambda b, h, i: (ctx_blk0 + b, vcol + h)),
    ]
    args = [lam_vec, q_gain, k_gain, subln, p, p, p]
    if with_latent:
        in_specs += [
            pl.BlockSpec((SEQ, LANE), lambda b, h, i: (b, kcol + h)),
            pl.BlockSpec((SEQ, LANE), lambda b, h, i: (b, vcol + h)),
        ] + [pl.BlockSpec((qt, LANE), lambda b, h, i: (i, 0))] * 3 + [
            pl.BlockSpec((SEQ, LANE), small)] * 3
        args += [p, p] + list(rope_tabs) + list(rope_tabs)
    n_rows = nb * (SEQ if with_latent else CTX_LEN)
    return pl.pallas_call(
        functools.partial(_attn_kernel, with_latent=with_latent, lambda_init=lambda_init),
        grid=(nb, A_HEADS, nq),
        in_specs=in_specs,
        out_specs=pl.BlockSpec((qt, LANE), lambda b, h, i: (b * nq + i, h)),
        out_shape=jax.ShapeDtypeStruct((n_rows, A_HEADS * A_DV), F32),
        scratch_shapes=[pltpu.VMEM((n_keys, LANE), BF16), pltpu.VMEM((n_keys, 2 * LANE), BF16)],
        compiler_params=_params(3),
        name="diff_attn_latent" if with_latent else "diff_attn_ctx",
    )(*args)


def _rope_tables():
    t = jnp.arange(SEQ, dtype=jnp.int32)
    pos = jnp.stack([(t // GRID_W).astype(F32), (t % GRID_W).astype(F32)], axis=1)
    n_freq = A_DQK // 4
    inv = ROPE_BASE ** (-jnp.arange(n_freq, dtype=F32) / n_freq)
    ang = pos[:, :, None] * inv
    cos = jnp.cos(ang)[:, None, :, None, :]
    sin = jnp.sin(ang)[:, None, :, None, :]
    shape = (SEQ, 2, 2, 2, n_freq)
    half = jnp.arange(2).reshape(1, 1, 1, 2, 1)
    cos_t = jnp.broadcast_to(cos, shape).reshape(SEQ, LANE)
    sin_lo = jnp.where(half == 0, -jnp.broadcast_to(sin, shape), 0.0).reshape(SEQ, LANE)
    sin_hi = jnp.where(half == 1, jnp.broadcast_to(sin, shape), 0.0).reshape(SEQ, LANE)
    return cos_t, sin_lo, sin_hi


def _hyena_filter_spectra(length, w1, b1, freq, w2, b2, w3):
    t = jnp.linspace(0.0, 1.0, length, dtype=F32)[:, None]
    w = 2.0 * math.pi * jnp.arange(length, dtype=F32)[:, None] / length
    bands = jnp.linspace(1e-4, HY_BANDS - 1, HY_BANDS, dtype=F32)[None]
    feats = jnp.concatenate([t, jnp.cos(bands * w), -jnp.sin(bands * w)], axis=-1)
    hdn = jnp.sin(freq[0] * (feats @ w1 + b1))
    hdn = jnp.sin(freq[1] * (hdn @ w2 + b2))
    h = (hdn @ w3).reshape(length, HY_ORDER, 2, HY_W)
    deltas = jnp.abs(jnp.linspace(HY_MIN_DECAY, HY_MAX_DECAY, HY_W, dtype=F32))
    h = h * jnp.exp(-t[:, :, None, None] * deltas)
    taps = jnp.concatenate([h[:, :, 0], jnp.zeros((1, HY_ORDER, HY_W), F32), h[:0:-1, :, 1]], axis=0)
    taps = taps / jnp.sum(jnp.abs(taps), axis=0, keepdims=True)
    return jnp.fft.rfft(taps, n=2 * length, axis=0)


def _dft_tables(n, period):
    idx = jnp.arange(n, dtype=jnp.int32)
    ang = ((idx[:, None] * idx[None, :]) % period).astype(F32) * (2.0 * math.pi / period)
    return jnp.cos(ang), jnp.sin(ang)


def _resident(shape, index_map):
    return pl.BlockSpec(shape, index_map, pipeline_mode=pl.Buffered(1))


HY_CB = 256
HY_ROWS = 512
HALO = 8


def _hyena_fold_kernel(*refs, length, rows):
    x1_refs, x2_refs, v_refs = refs[0:4], refs[4:8], refs[8:12]
    (cw1_ref, cw2_ref, cwv_ref, kt_ref, km_ref, bias_ref, cfe_ref, sfe_ref, cfo_ref, sfo_ref, cfot_ref,
     sfot_ref, oe_ref, oo_ref, pe_ref, po_ref, ze_ref, zo_ref, zeb_ref, zob_ref, ya_ref, yb_ref, yc_ref,
     yd_ref) = refs[12:]
    half = length // 2
    starts = range(0, half, rows)
    zeros = jnp.zeros((HALO, HY_CB), F32)
    for ref in (pe_ref, po_ref):
        ref[0:HALO, :] = zeros
        ref[HALO + half:, :] = zeros

    def load_split(part_refs):
        for dst_ref, lo_ref, hi_ref in ((pe_ref, part_refs[0], part_refs[1]),
                                        (po_ref, part_refs[2], part_refs[3])):
            dst_ref[HALO:HALO + half, 0:LANE] = lo_ref[...]
            dst_ref[HALO:HALO + half, LANE:] = hi_ref[...]

    def short_conv(w_ref, r0):
        w = w_ref[...]
        lo, hi = r0 + HALO, r0 + HALO + rows
        xe, xo = pe_ref[lo:hi, :], po_ref[lo:hi, :]
        even = po_ref[lo - 1:hi - 1, :] * w[0:1] + xe * w[1:2] + xo * w[2:3]
        odd = xe * w[0:1] + xo * w[1:2] + pe_ref[lo + 1:hi + 1, :] * w[2:3]
        return even, odd

    def put_z(r0, even, odd):
        ze_ref[r0:r0 + rows, :] = even
        zo_ref[r0:r0 + rows, :] = odd
        zeb_ref[r0:r0 + rows, :] = even.astype(BF16)
        zob_ref[r0:r0 + rows, :] = odd.astype(BF16)

    j = lax.broadcasted_iota(jnp.int32, (rows, 1), 0)
    alt = (1 - 2 * (j & 1)).astype(F32)

    def mm(t_ref, r0, x):
        return jnp.dot(t_ref[r0:r0 + rows, :], x, preferred_element_type=F32)

    load_split(v_refs)
    for r0 in starts:
        put_z(r0, *short_conv(cwv_ref, r0))

    for n, (gate_refs, gw_ref) in enumerate(((x1_refs, cw1_ref), (x2_refs, cw2_ref))):
        load_split(gate_refs)
        zr_mid = jnp.zeros((1, HY_CB), F32)
        zi_mid = jnp.zeros((1, HY_CB), F32)
        for r0 in starts:
            zeb, zob = zeb_ref[...], zob_ref[...]
            a, b = mm(cfe_ref, r0, zeb), mm(cfo_ref, r0, zob)
            a_s, b_s = mm(sfe_ref, r0, zeb), mm(sfo_ref, r0, zob)
            blk = slice(r0, r0 + rows)
            kr_lo, ki_lo, kr_hi, ki_hi = (kt_ref[n, i, blk, :] for i in range(4))
            zr, zi = a + b, a_s + b_s
            yr_lo, yi_lo = zr * kr_lo + zi * ki_lo, zr * ki_lo - zi * kr_lo
            zr, zi = a - b, b_s - a_s
            yr_hi, yi_hi = zr * kr_hi + zi * ki_hi, zr * ki_hi - zi * kr_hi
            ya_ref[blk, :] = (yr_lo + yr_hi).astype(BF16)
            yb_ref[blk, :] = (yi_lo - yi_hi).astype(BF16)
            yc_ref[blk, :] = (yr_lo - yr_hi).astype(BF16)
            yd_ref[blk, :] = (yi_lo + yi_hi).astype(BF16)
            zr_mid = zr_mid + jnp.sum(ze_ref[blk, :] * alt, axis=0, keepdims=True)
            zi_mid = zi_mid + jnp.sum(zo_ref[blk, :] * alt, axis=0, keepdims=True)
        kr_mid, ki_mid = km_ref[2 * n:2 * n + 1, :], km_ref[2 * n + 1:2 * n + 2, :]
        yr_mid = zr_mid * kr_mid + zi_mid * ki_mid
        yi_mid = zr_mid * ki_mid - zi_mid * kr_mid
        bias = bias_ref[n:n + 1, :]
        for r0 in starts:
            blk = slice(r0, r0 + rows)
            y_even = mm(cfe_ref, r0, ya_ref[...]) - mm(sfe_ref, r0, yb_ref[...]) + alt * yr_mid
            y_odd = mm(cfot_ref, r0, yc_ref[...]) - mm(sfot_ref, r0, yd_ref[...]) - alt * yi_mid
            g_even, g_odd = short_conv(gw_ref, r0)
            put_z(r0, g_even * (y_even + bias * ze_ref[blk, :]), g_odd * (y_odd + bias * zo_ref[blk, :]))

    oe_ref[...] = ze_ref[...]
    oo_ref[...] = zo_ref[...]


def _hyena_fold_tables(length):
    half = length // 2
    f = jnp.arange(half, dtype=jnp.int32)[:, None]
    j = jnp.arange(half, dtype=jnp.int32)[None, :]
    unit = math.pi / length
    even = ((f * 2 * j) % (2 * length)).astype(F32) * unit
    odd = ((f * (2 * j + 1)) % (2 * length)).astype(F32) * unit
    cfo, sfo = jnp.cos(odd), jnp.sin(odd)
    return tuple(t.astype(BF16) for t in (jnp.cos(even), jnp.sin(even), cfo, sfo, cfo.T, sfo.T))


def _hyena_fold(p_all, conv_w, kf, bias, tables, length, row_blk0, nb):
    half = length // 2
    wt = jnp.full((half, 1, 1), 1.0 / length, F32).at[0].set(0.5 / length)
    lo, hi = kf[:half], kf[length:half:-1]
    kt = jnp.stack([jnp.real(lo) * wt, jnp.imag(lo) * wt, jnp.real(hi) * wt, jnp.imag(hi) * wt], axis=0)
    kt = jnp.transpose(kt, (2, 0, 1, 3))
    mid = kf[half] * (1.0 / length)
    km = jnp.stack([jnp.real(mid), jnp.imag(mid)], axis=1).reshape(2 * HY_ORDER, HY_W)
    per = HY_W // HY_CB
    tiles = HY_CB // LANE
    p_pairs = p_all.reshape(p_all.shape[0] // 2, 2 * N_IN_PAD)

    def part(k):
        col = (OFF_HY + k * HY_W) // LANE
        return [pl.BlockSpec((half, LANE), lambda c, b, o=par * (N_IN_PAD // LANE) + col + t:
                             (row_blk0 + b, o + c * tiles))
                for par in range(2) for t in range(tiles)]

    cw = lambda k: _resident((HY_SHORT, HY_CB), lambda c, b: (0, k * per + c))
    table = _resident((half, half), lambda c, b: (0, 0))
    out_spec = pl.BlockSpec((half, HY_CB), lambda c, b: (b, c))
    out_shape = jax.ShapeDtypeStruct((nb * half, HY_W), F32)
    y_even, y_odd = pl.pallas_call(
        functools.partial(_hyena_fold_kernel, length=length, rows=min(half, HY_ROWS)),
        grid=(per, nb),
        in_specs=part(0) + part(1) + part(2) + [
            cw(0), cw(1), cw(2),
            _resident((HY_ORDER, 4, half, HY_CB), lambda c, b: (0, 0, 0, c)),
            _resident((2 * HY_ORDER, HY_CB), lambda c, b: (0, c)),
            _resident((HY_ORDER, HY_CB), lambda c, b: (0, c))] + [table] * 6,
        out_specs=[out_spec, out_spec],
        out_shape=[out_shape, out_shape],
        scratch_shapes=[pltpu.VMEM((half + 2 * HALO, HY_CB), F32)] * 2
        + [pltpu.VMEM((half, HY_CB), F32)] * 2 + [pltpu.VMEM((half, HY_CB), BF16)] * 6,
        compiler_params=_params(2),
        name="hyena_%d" % length,
    )(*([p_pairs] * 12), conv_w, conv_w, conv_w, kt, km, bias, *tables)
    return jnp.stack([y_even, y_odd], axis=1).reshape(nb * length, HY_W)


FN_CB = 256
FN_GW = FN_W // FN_GROUPS


def _fnet_kernel(u_ref, cg_ref, sg_ref, cl_ref, sl_ref, o_ref, *, scale):
    u = u_ref[...].astype(BF16)
    uc = jnp.dot(u, cg_ref[...], preferred_element_type=F32).astype(BF16)
    us = jnp.dot(u, sg_ref[...], preferred_element_type=F32).astype(BF16)
    y = (jnp.dot(cl_ref[...], uc, preferred_element_type=F32)
         - jnp.dot(sl_ref[...], us, preferred_element_type=F32))
    o_ref[...] = y * scale


def _fnet(p_all, group_dft, seq_dft, length, row_blk0, nb):
    col0 = OFF_FN // FN_CB
    return pl.pallas_call(
        functools.partial(_fnet_kernel, scale=(length * FN_GW) ** -0.5),
        grid=(nb, FN_W // FN_CB),
        in_specs=[pl.BlockSpec((length, FN_CB), lambda b, c: (row_blk0 + b, col0 + c)),
                  _resident((FN_CB, FN_CB), lambda b, c: (0, 0)),
                  _resident((FN_CB, FN_CB), lambda b, c: (0, 0)),
                  _resident((length, length), lambda b, c: (0, 0)),
                  _resident((length, length), lambda b, c: (0, 0))],
        out_specs=pl.BlockSpec((length, FN_CB), lambda b, c: (b, c)),
        out_shape=jax.ShapeDtypeStruct((nb * length, FN_W), F32),
        compiler_params=_params(2),
        name="fnet_%d" % length,
    )(p_all, group_dft[0], group_dft[1], seq_dft[0], seq_dft[1])


def _fnet_group_tables():
    cg, sg = _dft_tables(FN_GW, FN_GW)
    eye = jnp.eye(FN_CB // FN_GW, dtype=F32)
    return jnp.kron(eye, cg).astype(BF16), jnp.kron(eye, sg).astype(BF16)


DN_ROWS = CTX_LEN + SEQ
DN_BLK = 256
DN_NCH = DN_ROWS // DN_BLK
DN_CTX_CH = CTX_LEN // DN_BLK
DN_UNROLL = 3
DN_LAT0 = CTX_LEN + 2 * HALO
DN_BASE_LOG2 = 3


def _softplus(x):
    return jnp.maximum(x, 0.0) + jnp.log1p(jnp.exp(-jnp.abs(x)))


def _deltanet_kernel(qc_ref, kc_ref, vc_ref, zc_ref, abc_ref, ql_ref, kl_ref, vl_ref, zl_ref, abl_ref,
                     cwq_ref, cwk_ref, cwv_ref, alog_ref, dtb_ref, nw_ref, yl_ref, yc_ref,
                     pad_ref, qn_ref, kn_ref, vn_ref, gate_ref, u_ref, wq_ref, ik_ref, gl_ref, o_ref):
    head = pl.program_id(1)
    cs = DN_BLK

    zeros = jnp.zeros((HALO, LANE), F32)
    pad_ref[0:HALO, :] = zeros
    pad_ref[HALO + CTX_LEN:DN_LAT0, :] = zeros
    pad_ref[DN_LAT0 + SEQ:, :] = zeros

    def conv_silu(c_ref, l_ref, w_ref):
        pad_ref[HALO:HALO + CTX_LEN, :] = c_ref[...]
        pad_ref[DN_LAT0:DN_LAT0 + SEQ, :] = l_ref[...]
        w = w_ref[...]
        parts = []
        for start, n in ((HALO, CTX_LEN), (DN_LAT0, SEQ)):
            y = (pad_ref[start - 1:start - 1 + n, :] * w[0:1] + pad_ref[start:start + n, :] * w[1:2]
                 + pad_ref[start + 1:start + 1 + n, :] * w[2:3])
            parts.append(y * jax.nn.sigmoid(y))
        return parts

    def l2n(x):
        return x * lax.rsqrt(jnp.sum(x * x, axis=-1, keepdims=True) + EPS)

    for dst_ref, (c_ref, l_ref, w_ref), norm, scale in (
            (qn_ref, (qc_ref, ql_ref, cwq_ref), True, DN_DK ** -0.5),
            (kn_ref, (kc_ref, kl_ref, cwk_ref), True, 1.0),
            (vn_ref, (vc_ref, vl_ref, cwv_ref), False, 1.0)):
        yc, yl = conv_silu(c_ref, l_ref, w_ref)
        if norm:
            yc, yl = l2n(yc) * scale, l2n(yl) * scale
        dst_ref[0:CTX_LEN, :] = yc
        dst_ref[CTX_LEN:, :] = yl

    lane = lax.broadcasted_iota(jnp.int32, (1, LANE), 1)
    for ab_ref, r0, n in ((abc_ref, 0, CTX_LEN), (abl_ref, CTX_LEN, SEQ)):
        ab = ab_ref[...]
        g_all = -jnp.exp(alog_ref[...]) * _softplus(ab + dtb_ref[...])
        b_all = jax.nn.sigmoid(ab)
        for d in range(2):
            for k, (src, base) in enumerate(((g_all, 0), (b_all, 2 * DN_HEADS))):
                col = jnp.sum(jnp.where(lane == base + d * DN_HEADS + head, src, 0.0), axis=-1,
                              keepdims=True)
                gate_ref[2 * d + k, r0:r0 + n, :] = jnp.broadcast_to(col, (n, LANE))

    nt = (((1,), (1,)), ((), ()))

    def mm(x, y):
        return jnp.dot(x.astype(BF16), y.astype(BF16), preferred_element_type=F32)

    def block_body(it, carry):
        ii = lax.broadcasted_iota(jnp.int32, (cs, cs), 0)
        jj = lax.broadcasted_iota(jnp.int32, (cs, cs), 1)
        eye = ii == jj
        incl = (ii >= jj, ii <= jj)
        tri = tuple(jnp.where(m, 1.0, 0.0).astype(BF16) for m in incl)
        blocks = [it * DN_UNROLL + j for j in range(DN_UNROLL)]
        rows = [pl.ds(pl.multiple_of(c * cs, cs), cs) for c in blocks]
        q = [qn_ref[r, :] for r in rows]
        k = [kn_ref[r, :] for r in rows]
        v = [vn_ref[r, :] for r in rows]
        k16 = [x.astype(BF16) for x in k]
        kk = [lax.dot_general(x, x, nt, preferred_element_type=F32) for x in k16]
        qk = [lax.dot_general(x.astype(BF16), y, nt, preferred_element_type=F32) for x, y in zip(q, k16)]
        chains = [(j, d) for j in range(DN_UNROLL) for d in range(2)]
        g = [gate_ref[2 * d, rows[j], :] for j, d in chains]
        beta = [gate_ref[2 * d + 1, rows[j], :] for j, d in chains]
        g_hi = [x.astype(BF16) for x in g]
        g_lo = [(x - h.astype(F32)).astype(BF16) for x, h in zip(g, g_hi)]
        gc = [jnp.dot(tri[d], h, preferred_element_type=F32) + jnp.dot(tri[d], l, preferred_element_type=F32)
              for (j, d), h, l in zip(chains, g_hi, g_lo)]
        gc_last = [x[cs - 1:cs, :] if d == 0 else x[0:1, :] for (j, d), x in zip(chains, gc)]
        decay, a_mat = [], []
        for (j, d), x, bt in zip(chains, gc, beta):
            gc2 = jnp.concatenate([x] * (cs // LANE), axis=1)
            gc_row = jnp.sum(jnp.where(eye, gc2, 0.0), axis=0, keepdims=True)
            dec = jnp.where(incl[d], jnp.exp(jnp.where(incl[d], gc2 - gc_row, 0.0)), 0.0)
            decay.append(dec)
            a_mat.append(jnp.where(eye, 0.0, kk[j] * jnp.concatenate([bt] * (cs // LANE), axis=1) * dec))
        same = (ii >> DN_BASE_LOG2) == (jj >> DN_BASE_LOG2)
        t_mat = [jnp.where(eye, 1.0, 0.0) for _ in chains]
        p_mat = [jnp.where(same, -a, 0.0) for a in a_mat]
        for lvl in range(DN_BASE_LOG2 - 1):
            out = [mm(p, jnp.concatenate([t, p], axis=1)) for t, p in zip(t_mat, p_mat)]
            t_mat = [t + o[:, 0:cs] for t, o in zip(t_mat, out)]
            p_mat = [o[:, cs:] for o in out]
        t_mat = [t + mm(p, t) for t, p in zip(t_mat, p_mat)]
        for s in range(DN_BASE_LOG2, cs.bit_length() - 1):
            e_mask = jnp.logical_and((ii >> (s + 1)) == (jj >> (s + 1)), (ii >> s) != (jj >> s))
            et = [mm(jnp.where(e_mask, a, 0.0), t) for a, t in zip(a_mat, t_mat)]
            t_mat = [t - mm(t, e) for t, e in zip(t_mat, et)]
        e_gc = [jnp.exp(x) for x in gc]
        uw = [mm(t, jnp.concatenate([v[j] * bt, k[j] * bt * e], axis=1))
              for (j, d), t, bt, e in zip(chains, t_mat, beta, e_gc)]
        for n, (j, d) in enumerate(chains):
            idx = d * DN_NCH + blocks[j]
            u_ref[d, rows[j], :] = uw[n][:, 0:LANE]
            wq_ref[idx, 0:cs, :] = uw[n][:, LANE:].astype(BF16)
            wq_ref[idx, cs:, :] = (q[j] * e_gc[n]).astype(BF16)
            ik_ref[idx, 0:cs, :] = (qk[j] * decay[n]).astype(BF16)
            ik_ref[idx, cs:, :] = (k[j] * jnp.exp(gc_last[n] - gc[n])).T.astype(BF16)
            gl_ref[idx] = jnp.broadcast_to(jnp.exp(gc_last[n]), (HALO, LANE))
        return carry

    lax.fori_loop(0, DN_NCH // DN_UNROLL, block_body, 0)

    def scan_body(n, states):
        rev_c = jnp.where(n < DN_CTX_CH, DN_CTX_CH - 1 - n, DN_NCH + DN_CTX_CH - 1 - n)
        blk = (n, rev_c)
        idx = [d * DN_NCH + blk[d] for d in range(2)]
        rows = [pl.ds(pl.multiple_of(blk[d] * cs, cs), cs) for d in range(2)]
        ws_qs = [jnp.dot(wq_ref[idx[d]], states[d].astype(BF16), preferred_element_type=F32)
                 for d in range(2)]
        v_new = [u_ref[d, rows[d], :] - ws_qs[d][0:cs] for d in range(2)]
        r = [jnp.dot(ik_ref[idx[d]], v_new[d].astype(BF16), preferred_element_type=F32) for d in range(2)]
        for d in range(2):
            o_ref[d, rows[d], :] = ws_qs[d][cs:] + r[d][0:cs]
        return tuple(states[d] * gl_ref[idx[d]][0:1, :] + r[d][cs:] for d in range(2))

    s0 = jnp.zeros((DN_DK, DN_DV), F32)
    lax.fori_loop(0, DN_NCH, scan_body, (s0, s0))

    for y_ref, z_ref, r0, n in ((yc_ref, zc_ref, 0, CTX_LEN), (yl_ref, zl_ref, CTX_LEN, SEQ)):
        o = o_ref[0, r0:r0 + n, :] + o_ref[1, r0:r0 + n, :]
        z = z_ref[...]
        y = o * lax.rsqrt(jnp.mean(o * o, axis=-1, keepdims=True) + EPS) * nw_ref[...]
        y_ref[...] = y * (z * jax.nn.sigmoid(z))


def _deltanet(p_all, conv_w, a_log, dt_bias, norm_w, nb):
    ctx_blk0 = nb * SEQ // CTX_LEN
    qcol = OFF_DN // LANE
    zcol = OFF_DZ // LANE
    abcol = OFF_DA // LANE
    lat = lambda col: pl.BlockSpec((SEQ, LANE), lambda b, h: (b, col + h))
    ctx = lambda col: pl.BlockSpec((CTX_LEN, LANE), lambda b, h: (ctx_blk0 + b, col + h))
    cw = lambda k: pl.BlockSpec((3, LANE), lambda b, h: (0, k * DN_HEADS + h))
    small = pl.BlockSpec((1, LANE), lambda b, h: (0, 0))
    pad = lambda v: jnp.pad(v.reshape(1, -1), ((0, 0), (0, LANE - v.size)))
    n_idx = 2 * DN_NCH
    return pl.pallas_call(
        _deltanet_kernel,
        grid=(nb, DN_HEADS),
        in_specs=[ctx(qcol), ctx(qcol + DN_HEADS), ctx(qcol + 2 * DN_HEADS), ctx(zcol),
                  pl.BlockSpec((CTX_LEN, LANE), lambda b, h: (ctx_blk0 + b, abcol)),
                  lat(qcol), lat(qcol + DN_HEADS), lat(qcol + 2 * DN_HEADS), lat(zcol),
                  pl.BlockSpec((SEQ, LANE), lambda b, h: (b, abcol)),
                  cw(0), cw(1), cw(2), small, small, small],
        out_specs=[pl.BlockSpec((SEQ, LANE), lambda b, h: (b, h)),
                   pl.BlockSpec((CTX_LEN, LANE), lambda b, h: (b, h))],
        out_shape=[jax.ShapeDtypeStruct((nb * SEQ, BR_W), F32),
                   jax.ShapeDtypeStruct((nb * CTX_LEN, BR_W), F32)],
        scratch_shapes=[
            pltpu.VMEM((DN_LAT0 + SEQ + HALO, LANE), F32),
            pltpu.VMEM((DN_ROWS, LANE), F32),
            pltpu.VMEM((DN_ROWS, LANE), F32),
            pltpu.VMEM((DN_ROWS, LANE), F32),
            pltpu.VMEM((4, DN_ROWS, LANE), F32),
            pltpu.VMEM((2, DN_ROWS, LANE), F32),
            pltpu.VMEM((n_idx, 2 * DN_BLK, LANE), BF16),
            pltpu.VMEM((n_idx, DN_BLK + DN_DK, DN_BLK), BF16),
            pltpu.VMEM((n_idx, HALO, LANE), F32),
            pltpu.VMEM((2, DN_ROWS, LANE), F32),
        ],
        compiler_params=_params(2),
        name="deltanet",
    )(*([p_all] * 10), conv_w, conv_w, conv_w, pad(a_log), pad(dt_bias), norm_w.reshape(1, LANE))


def kernel(x, c, ctx, c_ctx, w_ada, b_ada, norm1, norm2, w_in, attn_qk_gain, attn_lambda, attn_subln,
           hy_conv, hy_w1, hy_b1, hy_freq, hy_w2, hy_b2, hy_w3, hy_bias, dn_conv, dn_a_log,
           dn_dt_bias, dn_norm, w_gate, w_branch, w_o, w_mlp1, w_mlp2):
    x_all = jnp.concatenate([x.reshape(N_LAT, D_MODEL), ctx.reshape(N_CTX, D_MODEL)], axis=0)
    cond = jax.nn.silu(jnp.concatenate([c, c_ctx[None, :]], axis=0))
    rope_tabs = _rope_tables()
    dft_lat = _hyena_fold_tables(SEQ)
    dft_ctx = _hyena_fold_tables(CTX_LEN)
    fn_lat = tuple(t.astype(BF16) for t in _dft_tables(SEQ, SEQ))
    fn_ctx = tuple(t.astype(BF16) for t in _dft_tables(CTX_LEN, CTX_LEN))
    fn_group = _fnet_group_tables()

    for li in range(DEPTH):
        need_ctx = li < DEPTH - 1
        lambda_init = 0.8 - 0.6 * math.exp(-0.3 * li)
        mod = (cond @ w_ada[li] + b_ada[li]).reshape(BATCH + 1, 6, D_MODEL)
        mod = jnp.pad(mod, ((0, 0), (0, MOD_ROWS - 6), (0, 0)))
        w_in_pad = jnp.pad(w_in[li], ((0, 0), (0, N_IN_PAD - N_IN))).astype(BF16)

        p_all = _proj_in(x_all, mod, norm1[li][None, :], w_in_pad)

        q_gain = jnp.tile(attn_qk_gain[li, 0], 2)[None, :]
        k_gain = jnp.tile(attn_qk_gain[li, 1], 2)[None, :]
        subln = attn_subln[li][None, :]
        ya = _attention(p_all, attn_lambda[li], q_gain, k_gain, subln, rope_tabs, lambda_init, True, BATCH)

        hy_filt = (hy_w1[li], hy_b1[li], hy_freq[li], hy_w2[li], hy_b2[li], hy_w3[li])
        yb = _hyena_fold(p_all, hy_conv[li], _hyena_filter_spectra(SEQ, *hy_filt), hy_bias[li], dft_lat,
                         SEQ, 0, BATCH)
        yf = _fnet(p_all, fn_group, fn_lat, SEQ, 0, BATCH)
        yd, yd_c = _deltanet(p_all, dn_conv[li], dn_a_log[li], dn_dt_bias[li], dn_norm[li], BATCH)
        merge_w = (w_gate[li].astype(BF16), w_branch[li].astype(BF16), w_o[li].astype(BF16))
        x_all = _merge(x_all, mod, norm1[li][None, :], [ya, yb, yf, yd], *merge_w, 0, N_LAT)
        n_rows = N_LAT
        if need_ctx:
            ya_c = _attention(p_all, attn_lambda[li], q_gain, k_gain, subln, None, lambda_init, False,
                              BATCH)
            yb_c = _hyena_fold(p_all, hy_conv[li], _hyena_filter_spectra(CTX_LEN, *hy_filt), hy_bias[li],
                               dft_ctx, CTX_LEN, N_LAT // CTX_LEN, BATCH)
            yf_c = _fnet(p_all, fn_group, fn_ctx, CTX_LEN, N_LAT // CTX_LEN, BATCH)
            x_all = _merge(x_all, mod, norm1[li][None, :], [ya_c, yb_c, yf_c, yd_c],
                           *merge_w, N_LAT // ROW_TILE, N_CTX)
            n_rows = N_ALL
        x_all = _mlp(x_all, mod, norm2[li][None, :], w_mlp1[li].astype(BF16),
                     w_mlp2[li].astype(BF16), n_rows)
    return x_all[:N_LAT].reshape(BATCH, SEQ, D_MODEL)
```

```python
import functools
import math

import jax
import jax.numpy as jnp
from jax import lax
from jax.experimental import pallas as pl
from jax.experimental.pallas import tpu as pltpu

F32 = jnp.float32
BF16 = jnp.bfloat16

D_MODEL = 1024
BATCH = 16
SEQ = 2048
DEPTH = 4
GRID_W = 64
CTX_LEN = 256
N_BRANCH = 4
BR_W = 512
A_HEADS = 4
A_DQK = 64
A_DV = 2 * A_DQK
ROPE_BASE = 10000.0
HY_W = BR_W
HY_ORDER = 2
HY_SHORT = 3
HY_BANDS = 16
HY_HID = 64
HY_MIN_DECAY = math.log(1e-2) / 1.5
HY_MAX_DECAY = math.log(1e-2) / 0.3
FN_W = BR_W
FN_GROUPS = 4
DN_HEADS = 4
DN_DK = 128
DN_DV = 128
DN_CHUNK = 64
D_FF = 4 * D_MODEL
EPS = 1e-6

OFF_AQ = 0
OFF_AK = OFF_AQ + A_HEADS * 2 * A_DQK
OFF_AV = OFF_AK + A_HEADS * 2 * A_DQK
OFF_HY = OFF_AV + A_HEADS * A_DV
OFF_FN = OFF_HY + (HY_ORDER + 1) * HY_W
OFF_DN = OFF_FN + FN_W
OFF_DZ = OFF_DN + DN_HEADS * (2 * DN_DK + DN_DV)
OFF_DA = OFF_DZ + DN_HEADS * DN_DV
OFF_DB = OFF_DA + 2 * DN_HEADS
N_IN = OFF_DB + 2 * DN_HEADS

LANE = 128
N_IN_PAD = 45 * LANE
N_LAT = BATCH * SEQ
N_CTX = BATCH * CTX_LEN
N_ALL = N_LAT + N_CTX
MOD_ROWS = 8
VMEM_LIMIT = 56 * 1024 * 1024

ROW_TILE = 256
ATT_QT = 512
ATT_ROWS = 256


def _params(n_axes):
    return pltpu.CompilerParams(dimension_semantics=("arbitrary",) * n_axes,
                                vmem_limit_bytes=VMEM_LIMIT)


def _mod_row(i):
    return jnp.minimum(i // (SEQ // ROW_TILE), BATCH)


def _ada_norm(x, nw, shift, scale):
    ms = jnp.mean(x * x, axis=-1, keepdims=True)
    return (x * lax.rsqrt(ms + EPS) * nw) * (1.0 + scale) + shift


def _proj_in_kernel(x_ref, mod_ref, nw_ref, w_ref, p_ref):
    h = _ada_norm(x_ref[...], nw_ref[...], mod_ref[0, 0:1, :], mod_ref[0, 1:2, :])
    p_ref[...] = jnp.dot(h.astype(BF16), w_ref[...], preferred_element_type=F32)


def _proj_in(x_all, mod, nw, w_in_pad):
    n_tiles = N_ALL // ROW_TILE
    return pl.pallas_call(
        _proj_in_kernel,
        grid=(n_tiles,),
        in_specs=[
            pl.BlockSpec((ROW_TILE, D_MODEL), lambda i: (i, 0)),
            pl.BlockSpec((1, MOD_ROWS, D_MODEL), lambda i: (_mod_row(i), 0, 0)),
            pl.BlockSpec((1, D_MODEL), lambda i: (0, 0)),
            pl.BlockSpec((D_MODEL, N_IN_PAD), lambda i: (0, 0)),
        ],
        out_specs=pl.BlockSpec((ROW_TILE, N_IN_PAD), lambda i: (i, 0)),
        out_shape=jax.ShapeDtypeStruct((N_ALL, N_IN_PAD), F32),
        compiler_params=_params(1),
        name="proj_in",
    )(x_all, mod, nw, w_in_pad)


def _merge_kernel(x_ref, mod_ref, nw_ref, ya_ref, yb_ref, yf_ref, yd_ref, wg_ref, wb_ref, wo_ref,
                  o_ref):
    x = x_ref[...]
    h = _ada_norm(x, nw_ref[...], mod_ref[0, 0:1, :], mod_ref[0, 1:2, :]).astype(BF16)
    acc = None
    for n, y_ref in enumerate((ya_ref, yb_ref, yf_ref, yd_ref)):
        gate = jax.nn.sigmoid(jnp.dot(h, wg_ref[n], preferred_element_type=F32))
        br = jnp.dot(y_ref[...].astype(BF16), wb_ref[n], preferred_element_type=F32)
        acc = gate * br if acc is None else acc + gate * br
    out = jnp.dot(acc.astype(BF16), wo_ref[...], preferred_element_type=F32)
    o_ref[...] = x + mod_ref[0, 2:3, :] * out


def _merge(x_all, mod, nw, ys, wg, wb, wo, row_blk0, n_rows):
    xrow = lambda i: (row_blk0 + i, 0)
    yrow = lambda i: (i, 0)
    const2 = lambda i: (0, 0)
    const3 = lambda i: (0, 0, 0)
    return pl.pallas_call(
        _merge_kernel,
        grid=(n_rows // ROW_TILE,),
        in_specs=[
            pl.BlockSpec((ROW_TILE, D_MODEL), xrow),
            pl.BlockSpec((1, MOD_ROWS, D_MODEL), lambda i: (_mod_row(row_blk0 + i), 0, 0)),
            pl.BlockSpec((1, D_MODEL), const2),
        ] + [pl.BlockSpec((ROW_TILE, BR_W), yrow)] * N_BRANCH + [
            pl.BlockSpec((N_BRANCH, D_MODEL, D_MODEL), const3),
            pl.BlockSpec((N_BRANCH, BR_W, D_MODEL), const3),
            pl.BlockSpec((D_MODEL, D_MODEL), const2),
        ],
        out_specs=pl.BlockSpec((ROW_TILE, D_MODEL), xrow),
        out_shape=jax.ShapeDtypeStruct((N_ALL, D_MODEL), F32),
        input_output_aliases={0: 0},
        compiler_params=_params(1),
        name="merge",
    )(x_all, mod, nw, *ys, wg, wb, wo)


def _mlp_kernel(x_ref, mod_ref, nw_ref, w1_ref, w2_ref, o_ref):
    x = x_ref[...]
    h = _ada_norm(x, nw_ref[...], mod_ref[0, 3:4, :], mod_ref[0, 4:5, :]).astype(BF16)
    u = jnp.maximum(jnp.dot(h, w1_ref[...], preferred_element_type=F32), 0.0)
    out = jnp.dot((u * u).astype(BF16), w2_ref[...], preferred_element_type=F32)
    o_ref[...] = x + mod_ref[0, 5:6, :] * out


def _mlp(x_all, mod, nw, w1, w2, n_rows):
    row = lambda i: (i, 0)
    const2 = lambda i: (0, 0)
    return pl.pallas_call(
        _mlp_kernel,
        grid=(n_rows // ROW_TILE,),
        in_specs=[
            pl.BlockSpec((ROW_TILE, D_MODEL), row),
            pl.BlockSpec((1, MOD_ROWS, D_MODEL), lambda i: (_mod_row(i), 0, 0)),
            pl.BlockSpec((1, D_MODEL), const2),
            pl.BlockSpec((D_MODEL, D_FF), const2),
            pl.BlockSpec((D_FF, D_MODEL), const2),
        ],
        out_specs=pl.BlockSpec((ROW_TILE, D_MODEL), row),
        out_shape=jax.ShapeDtypeStruct((N_ALL, D_MODEL), F32),
        input_output_aliases={0: 0},
        compiler_params=_params(1),
        name="mlp",
    )(x_all, mod, nw, w1, w2)


def _qk_norm(x, gain):
    lane = lax.broadcasted_iota(jnp.int32, x.shape, 1)
    first = lane < A_DQK
    sq = x * x
    s_all = jnp.sum(sq, axis=-1, keepdims=True)
    s_first = jnp.sum(jnp.where(first, sq, 0.0), axis=-1, keepdims=True)
    ms = jnp.where(first, s_first, s_all - s_first) * (1.0 / A_DQK)
    return x * lax.rsqrt(ms + EPS) * gain


def _rope(x, cos, sin_lo, sin_hi):
    up = pltpu.roll(x, LANE - A_DQK // 4, 1)
    down = pltpu.roll(x, A_DQK // 4, 1)
    return x * cos + up * sin_lo + down * sin_hi


def _attn_kernel(lam_ref, qg_ref, kg_ref, sub_ref, q_ref, *rest, with_latent, lambda_init):
    if with_latent:
        (kc_ref, vc_ref, kl_ref, vl_ref, cq_ref, sq_lo_ref, sq_hi_ref,
         ck_ref, sk_lo_ref, sk_hi_ref, o_ref, kn_ref, vn_ref) = rest
    else:
        kc_ref, vc_ref, o_ref, kn_ref, vn_ref = rest

    @pl.when(pl.program_id(2) == 0)
    def _prep_keys():
        vn_ref[:, LANE:] = jnp.ones((vn_ref.shape[0], LANE), BF16)
        kn_ref[0:CTX_LEN, :] = _qk_norm(kc_ref[...], kg_ref[...]).astype(BF16)
        vn_ref[0:CTX_LEN, 0:LANE] = vc_ref[...].astype(BF16)
        if with_latent:
            kl = _qk_norm(kl_ref[...], kg_ref[...])
            kl = _rope(kl, ck_ref[...], sk_lo_ref[...], sk_hi_ref[...])
            kn_ref[CTX_LEN:, :] = kl.astype(BF16)
            vn_ref[CTX_LEN:, 0:LANE] = vl_ref[...].astype(BF16)

    q = _qk_norm(q_ref[...], qg_ref[...])
    if with_latent:
        q = _rope(q, cq_ref[...], sq_lo_ref[...], sq_hi_ref[...])
    q = q * (A_DQK ** -0.5 * math.log2(math.e))
    lane = lax.broadcasted_iota(jnp.int32, q.shape, 1)
    first = lane < A_DQK
    kn = kn_ref[...]
    vn = vn_ref[...]

    n_grp = max(q.shape[0] // ATT_ROWS, 1)
    rows = q.shape[0] // n_grp
    qs = [jnp.where(first if c == 0 else jnp.logical_not(first), q, 0.0)[g * rows:(g + 1) * rows]
          for g in range(n_grp) for c in range(2)]
    s = [lax.dot_general(x.astype(BF16), kn, (((1,), (1,)), ((), ())), preferred_element_type=F32)
         for x in qs]
    e = [jnp.exp2((x - jnp.max(x, axis=-1, keepdims=True)).astype(BF16)) for x in s]
    ov = [jnp.dot(x, vn, preferred_element_type=F32) for x in e]
    oc = [x[:, 0:LANE] / x[:, LANE:LANE + 1] for x in ov]

    lf = lam_ref[...]
    lam = (jnp.exp(jnp.sum(lf[0:1] * lf[1:2], axis=-1, keepdims=True))
           - jnp.exp(jnp.sum(lf[2:3] * lf[3:4], axis=-1, keepdims=True)) + lambda_init)
    o = jnp.concatenate([oc[2 * g] - lam * oc[2 * g + 1] for g in range(n_grp)], axis=0)
    ms = jnp.mean(o * o, axis=-1, keepdims=True)
    o_ref[...] = (o * lax.rsqrt(ms + EPS) * sub_ref[...]) * (1.0 - lambda_init)


def _attention(p, lam_vec, q_gain, k_gain, subln, rope_tabs, lambda_init, with_latent, nb):
    kcol = OFF_AK // LANE
    vcol = OFF_AV // LANE
    ctx_blk0 = nb * SEQ // CTX_LEN
    small = lambda b, h, i: (0, 0)
    if with_latent:
        qt, nq = ATT_QT, SEQ // ATT_QT
        q_spec = pl.BlockSpec((qt, LANE), lambda b, h, i: (b * nq + i, h))
        n_keys = CTX_LEN + SEQ
    else:
        qt, nq = CTX_LEN, 1
        q_spec = pl.BlockSpec((qt, LANE), lambda b, h, i: (ctx_blk0 + b, h))
        n_keys = CTX_LEN
    in_specs = [
        pl.BlockSpec((4, A_DQK), small),
        pl.BlockSpec((1, LANE), small),
        pl.BlockSpec((1, LANE), small),
        pl.BlockSpec((1, LANE), small),
        q_spec,
        pl.BlockSpec((CTX_LEN, LANE), lambda b, h, i: (ctx_blk0 + b, kcol + h)),
        pl.BlockSpec((CTX_LEN, LANE), lambda b, h, i: (ctx_blk0 + b, vcol + h)),
    ]
    args = [lam_vec, q_gain, k_gain, subln, p, p, p]
    if with_latent:
        in_specs += [
            pl.BlockSpec((SEQ, LANE), lambda b, h, i: (b, kcol + h)),
            pl.BlockSpec((SEQ, LANE), lambda b, h, i: (b, vcol + h)),
        ] + [pl.BlockSpec((qt, LANE), lambda b, h, i: (i, 0))] * 3 + [
            pl.BlockSpec((SEQ, LANE), small)] * 3
        args += [p, p] + list(rope_tabs) + list(rope_tabs)
    n_rows = nb * (SEQ if with_latent else CTX_LEN)
    return pl.pallas_call(
        functools.partial(_attn_kernel, with_latent=with_latent, lambda_init=lambda_init),
        grid=(nb, A_HEADS, nq),
        in_specs=in_specs,
        out_specs=pl.BlockSpec((qt, LANE), lambda b, h, i: (b * nq + i, h)),
        out_shape=jax.ShapeDtypeStruct((n_rows, A_HEADS * A_DV), F32),
        scratch_shapes=[pltpu.VMEM((n_keys, LANE), BF16), pltpu.VMEM((n_keys, 2 * LANE), BF16)],
        compiler_params=_params(3),
        name="diff_attn_latent" if with_latent else "diff_attn_ctx",
    )(*args)


def _rope_tables():
    t = jnp.arange(SEQ, dtype=jnp.int32)
    pos = jnp.stack([(t // GRID_W).astype(F32), (t % GRID_W).astype(F32)], axis=1)
    n_freq = A_DQK // 4
    inv = ROPE_BASE ** (-jnp.arange(n_freq, dtype=F32) / n_freq)
    ang = pos[:, :, None] * inv
    cos = jnp.cos(ang)[:, None, :, None, :]
    sin = jnp.sin(ang)[:, None, :, None, :]
    shape = (SEQ, 2, 2, 2, n_freq)
    half = jnp.arange(2).reshape(1, 1, 1, 2, 1)
    cos_t = jnp.broadcast_to(cos, shape).reshape(SEQ, LANE)
    sin_lo = jnp.where(half == 0, -jnp.broadcast_to(sin, shape), 0.0).reshape(SEQ, LANE)
    sin_hi = jnp.where(half == 1, jnp.broadcast_to(sin, shape), 0.0).reshape(SEQ, LANE)
    return cos_t, sin_lo, sin_hi


def _hyena_filter_spectra(length, w1, b1, freq, w2, b2, w3):
    t = jnp.linspace(0.0, 1.0, length, dtype=F32)[:, None]
    w = 2.0 * math.pi * jnp.arange(length, dtype=F32)[:, None] / length
    bands = jnp.linspace(1e-4, HY_BANDS - 1, HY_BANDS, dtype=F32)[None]
    feats = jnp.concatenate([t, jnp.cos(bands * w), -jnp.sin(bands * w)], axis=-1)
    hdn = jnp.sin(freq[0] * (feats @ w1 + b1))
    hdn = jnp.sin(freq[1] * (hdn @ w2 + b2))
    h = (hdn @ w3).reshape(length, HY_ORDER, 2, HY_W)
    deltas = jnp.abs(jnp.linspace(HY_MIN_DECAY, HY_MAX_DECAY, HY_W, dtype=F32))
    h = h * jnp.exp(-t[:, :, None, None] * deltas)
    past = h[:, :, 0].reshape(length, HY_ORDER * HY_W)
    fut = h[:, :, 1].at[0].set(0.0).reshape(length, HY_ORDER * HY_W)
    norm = jnp.sum(jnp.abs(past) + jnp.abs(fut), axis=0, keepdims=True)
    return _tap_spectrum((past + fut) / norm, (past - fut) / norm, length)


def _tap_spectrum_kernel(sum_ref, dif_ref, c_ref, s_ref, kr_ref, ki_ref, kn_ref):
    a = sum_ref[...]
    kr_ref[...] = jnp.dot(c_ref[...], a.astype(BF16), preferred_element_type=F32)
    ki_ref[...] = -jnp.dot(s_ref[...], dif_ref[...].astype(BF16), preferred_element_type=F32)
    t = lax.broadcasted_iota(jnp.int32, (a.shape[0], 1), 0)
    nyq = jnp.sum(a * (1 - 2 * (t & 1)).astype(F32), axis=0, keepdims=True)
    kn_ref[...] = jnp.broadcast_to(nyq, kn_ref.shape)


def _tap_spectrum(tap_sum, tap_dif, length):
    n = tap_sum.shape[1]
    cos, sin = (t.astype(BF16) for t in _dft_tables(length, 2 * length))
    col = pl.BlockSpec((length, HY_CB), lambda c: (0, c))
    table = _resident((length, length), lambda c: (0, 0))
    kr, ki, kn = pl.pallas_call(
        _tap_spectrum_kernel,
        grid=(n // HY_CB,),
        in_specs=[col, col, table, table],
        out_specs=[col, col, pl.BlockSpec((HALO, HY_CB), lambda c: (0, c))],
        out_shape=[jax.ShapeDtypeStruct((length, n), F32)] * 2 + [jax.ShapeDtypeStruct((HALO, n), F32)],
        compiler_params=_params(1),
        name="tap_spectrum_%d" % length,
    )(tap_sum, tap_dif, cos, sin)
    shape = (length + 1, HY_ORDER, HY_W)
    return (jnp.concatenate([kr, kn[0:1]], axis=0).reshape(shape),
            jnp.concatenate([ki, jnp.zeros((1, n), F32)], axis=0).reshape(shape))


def _dft_tables(n, period):
    idx = jnp.arange(n, dtype=jnp.int32)
    ang = ((idx[:, None] * idx[None, :]) % period).astype(F32) * (2.0 * math.pi / period)
    return jnp.cos(ang), jnp.sin(ang)


def _resident(shape, index_map):
    return pl.BlockSpec(shape, index_map, pipeline_mode=pl.Buffered(1))


HY_CB = 256
HY_ROWS = 512
HALO = 8


def _hyena_fold_kernel(*refs, length, rows):
    tiles = HY_CB // LANE
    x1_refs, x2_refs, v_refs = refs[0:tiles], refs[tiles:2 * tiles], refs[2 * tiles:3 * tiles]
    (cw1_ref, cw2_ref, cwv_ref, kt_ref, km_ref, bias_ref, cfe_ref, sfe_ref, cfo_ref, sfo_ref, cfot_ref,
     sfot_ref, oe_ref, oo_ref, pe_ref, po_ref, ze_ref, zo_ref, zeb_ref, zob_ref, ya_ref, yb_ref, yc_ref,
     yd_ref) = refs[3 * tiles:]
    half = length // 2
    starts = range(0, half, rows)
    zeros = jnp.zeros((HALO, HY_CB), F32)
    for ref in (pe_ref, po_ref):
        ref[0:HALO, :] = zeros
        ref[HALO + half:, :] = zeros

    def load_split(part_refs):
        for t, ref in enumerate(part_refs):
            pe_ref[HALO:HALO + half, t * LANE:(t + 1) * LANE] = ref[pl.ds(0, half, stride=2), :]
            po_ref[HALO:HALO + half, t * LANE:(t + 1) * LANE] = ref[pl.ds(1, half, stride=2), :]

    def short_conv(w_ref, r0):
        w = w_ref[...]
        lo, hi = r0 + HALO, r0 + HALO + rows
        xe, xo = pe_ref[lo:hi, :], po_ref[lo:hi, :]
        even = po_ref[lo - 1:hi - 1, :] * w[0:1] + xe * w[1:2] + xo * w[2:3]
        odd = xe * w[0:1] + xo * w[1:2] + pe_ref[lo + 1:hi + 1, :] * w[2:3]
        return even, odd

    def put_z(r0, even, odd):
        ze_ref[r0:r0 + rows, :] = even
        zo_ref[r0:r0 + rows, :] = odd
        zeb_ref[r0:r0 + rows, :] = even.astype(BF16)
        zob_ref[r0:r0 + rows, :] = odd.astype(BF16)

    j = lax.broadcasted_iota(jnp.int32, (rows, 1), 0)
    alt = (1 - 2 * (j & 1)).astype(F32)

    def mm(t_ref, r0, x):
        return jnp.dot(t_ref[r0:r0 + rows, :], x, preferred_element_type=F32)

    load_split(v_refs)
    for r0 in starts:
        put_z(r0, *short_conv(cwv_ref, r0))

    for n, (gate_refs, gw_ref) in enumerate(((x1_refs, cw1_ref), (x2_refs, cw2_ref))):
        load_split(gate_refs)
        zr_mid = jnp.zeros((1, HY_CB), F32)
        zi_mid = jnp.zeros((1, HY_CB), F32)
        for r0 in starts:
            zeb, zob = zeb_ref[...], zob_ref[...]
            a, b = mm(cfe_ref, r0, zeb), mm(cfo_ref, r0, zob)
            a_s, b_s = mm(sfe_ref, r0, zeb), mm(sfo_ref, r0, zob)
            blk = slice(r0, r0 + rows)
            kr_lo, ki_lo, kr_hi, ki_hi = (kt_ref[n, i, blk, :] for i in range(4))
            zr, zi = a + b, a_s + b_s
            yr_lo, yi_lo = zr * kr_lo + zi * ki_lo, zr * ki_lo - zi * kr_lo
            zr, zi = a - b, b_s - a_s
            yr_hi, yi_hi = zr * kr_hi + zi * ki_hi, zr * ki_hi - zi * kr_hi
            ya_ref[blk, :] = (yr_lo + yr_hi).astype(BF16)
            yb_ref[blk, :] = (yi_lo - yi_hi).astype(BF16)
            yc_ref[blk, :] = (yr_lo - yr_hi).astype(BF16)
            yd_ref[blk, :] = (yi_lo + yi_hi).astype(BF16)
            zr_mid = zr_mid + jnp.sum(ze_ref[blk, :] * alt, axis=0, keepdims=True)
            zi_mid = zi_mid + jnp.sum(zo_ref[blk, :] * alt, axis=0, keepdims=True)
        kr_mid, ki_mid = km_ref[2 * n:2 * n + 1, :], km_ref[2 * n + 1:2 * n + 2, :]
        yr_mid = zr_mid * kr_mid + zi_mid * ki_mid
        yi_mid = zr_mid * ki_mid - zi_mid * kr_mid
        bias = bias_ref[n:n + 1, :]
        for r0 in starts:
            blk = slice(r0, r0 + rows)
            y_even = mm(cfe_ref, r0, ya_ref[...]) - mm(sfe_ref, r0, yb_ref[...]) + alt * yr_mid
            y_odd = mm(cfot_ref, r0, yc_ref[...]) - mm(sfot_ref, r0, yd_ref[...]) - alt * yi_mid
            g_even, g_odd = short_conv(gw_ref, r0)
            put_z(r0, g_even * (y_even + bias * ze_ref[blk, :]), g_odd * (y_odd + bias * zo_ref[blk, :]))

    oe_ref[...] = ze_ref[...]
    oo_ref[...] = zo_ref[...]


def _hyena_fold_tables(length):
    half = length // 2
    f = jnp.arange(half, dtype=jnp.int32)[:, None]
    j = jnp.arange(half, dtype=jnp.int32)[None, :]
    unit = math.pi / length
    even = ((f * 2 * j) % (2 * length)).astype(F32) * unit
    odd = ((f * (2 * j + 1)) % (2 * length)).astype(F32) * unit
    cfo, sfo = jnp.cos(odd), jnp.sin(odd)
    return tuple(t.astype(BF16) for t in (jnp.cos(even), jnp.sin(even), cfo, sfo, cfo.T, sfo.T))


def _hyena_fold(p_all, conv_w, kf, bias, tables, length, row_blk0, nb):
    half = length // 2
    k_re, k_im = kf
    wt = jnp.full((half, 1, 1), 1.0 / length, F32).at[0].set(0.5 / length)
    kt = jnp.stack([k_re[:half] * wt, k_im[:half] * wt,
                    k_re[length:half:-1] * wt, k_im[length:half:-1] * wt], axis=0)
    kt = jnp.transpose(kt, (2, 0, 1, 3))
    km = jnp.stack([k_re[half], k_im[half]], axis=1).reshape(2 * HY_ORDER, HY_W) * (1.0 / length)
    per = HY_W // HY_CB
    tiles = HY_CB // LANE

    def part(k):
        col = (OFF_HY + k * HY_W) // LANE
        return [pl.BlockSpec((length, LANE), lambda c, b, o=col + t: (row_blk0 + b, o + c * tiles))
                for t in range(tiles)]

    cw = lambda k: _resident((HY_SHORT, HY_CB), lambda c, b: (0, k * per + c))
    table = _resident((half, half), lambda c, b: (0, 0))
    out_spec = pl.BlockSpec((half, HY_CB), lambda c, b: (b, c))
    out_shape = jax.ShapeDtypeStruct((nb * half, HY_W), F32)
    y_even, y_odd = pl.pallas_call(
        functools.partial(_hyena_fold_kernel, length=length, rows=min(half, HY_ROWS)),
        grid=(per, nb),
        in_specs=part(0) + part(1) + part(2) + [
            cw(0), cw(1), cw(2),
            _resident((HY_ORDER, 4, half, HY_CB), lambda c, b: (0, 0, 0, c)),
            _resident((2 * HY_ORDER, HY_CB), lambda c, b: (0, c)),
            _resident((HY_ORDER, HY_CB), lambda c, b: (0, c))] + [table] * 6,
        out_specs=[out_spec, out_spec],
        out_shape=[out_shape, out_shape],
        scratch_shapes=[pltpu.VMEM((half + 2 * HALO, HY_CB), F32)] * 2
        + [pltpu.VMEM((half, HY_CB), F32)] * 2 + [pltpu.VMEM((half, HY_CB), BF16)] * 6,
        compiler_params=_params(2),
        name="hyena_%d" % length,
    )(*([p_all] * (3 * tiles)), conv_w, conv_w, conv_w, kt, km, bias, *tables)
    return jnp.stack([y_even, y_odd], axis=1).reshape(nb * length, HY_W)


FN_CB = 256
FN_GW = FN_W // FN_GROUPS


def _fnet_kernel(u_ref, cg_ref, sg_ref, cl_ref, sl_ref, o_ref, *, scale):
    u = u_ref[...].astype(BF16)
    uc = jnp.dot(u, cg_ref[...], preferred_element_type=F32).astype(BF16)
    us = jnp.dot(u, sg_ref[...], preferred_element_type=F32).astype(BF16)
    y = (jnp.dot(cl_ref[...], uc, preferred_element_type=F32)
         - jnp.dot(sl_ref[...], us, preferred_element_type=F32))
    o_ref[...] = y * scale


def _fnet(p_all, group_dft, seq_dft, length, row_blk0, nb):
    col0 = OFF_FN // FN_CB
    return pl.pallas_call(
        functools.partial(_fnet_kernel, scale=(length * FN_GW) ** -0.5),
        grid=(nb, FN_W // FN_CB),
        in_specs=[pl.BlockSpec((length, FN_CB), lambda b, c: (row_blk0 + b, col0 + c)),
                  _resident((FN_CB, FN_CB), lambda b, c: (0, 0)),
                  _resident((FN_CB, FN_CB), lambda b, c: (0, 0)),
                  _resident((length, length), lambda b, c: (0, 0)),
                  _resident((length, length), lambda b, c: (0, 0))],
        out_specs=pl.BlockSpec((length, FN_CB), lambda b, c: (b, c)),
        out_shape=jax.ShapeDtypeStruct((nb * length, FN_W), F32),
        compiler_params=_params(2),
        name="fnet_%d" % length,
    )(p_all, group_dft[0], group_dft[1], seq_dft[0], seq_dft[1])


def _fnet_group_tables():
    cg, sg = _dft_tables(FN_GW, FN_GW)
    eye = jnp.eye(FN_CB // FN_GW, dtype=F32)
    return jnp.kron(eye, cg).astype(BF16), jnp.kron(eye, sg).astype(BF16)


DN_ROWS = CTX_LEN + SEQ
DN_BLK = 256
DN_NCH = DN_ROWS // DN_BLK
DN_CTX_CH = CTX_LEN // DN_BLK
DN_UNROLL = 3
DN_LAT0 = CTX_LEN + 2 * HALO
DN_BASE_LOG2 = 3


def _softplus(x):
    return jnp.maximum(x, 0.0) + jnp.log1p(jnp.exp(-jnp.abs(x)))


def _deltanet_kernel(qc_ref, kc_ref, vc_ref, zc_ref, abc_ref, ql_ref, kl_ref, vl_ref, zl_ref, abl_ref,
                     cwq_ref, cwk_ref, cwv_ref, alog_ref, dtb_ref, nw_ref, yl_ref, yc_ref,
                     pad_ref, qn_ref, kn_ref, vn_ref, gate_ref, u_ref, wq_ref, ik_ref, gl_ref, o_ref):
    head = pl.program_id(1)
    cs = DN_BLK

    zeros = jnp.zeros((HALO, LANE), F32)
    pad_ref[0:HALO, :] = zeros
    pad_ref[HALO + CTX_LEN:DN_LAT0, :] = zeros
    pad_ref[DN_LAT0 + SEQ:, :] = zeros

    def conv_silu(c_ref, l_ref, w_ref):
        pad_ref[HALO:HALO + CTX_LEN, :] = c_ref[...]
        pad_ref[DN_LAT0:DN_LAT0 + SEQ, :] = l_ref[...]
        w = w_ref[...]
        parts = []
        for start, n in ((HALO, CTX_LEN), (DN_LAT0, SEQ)):
            y = (pad_ref[start - 1:start - 1 + n, :] * w[0:1] + pad_ref[start:start + n, :] * w[1:2]
                 + pad_ref[start + 1:start + 1 + n, :] * w[2:3])
            parts.append(y * jax.nn.sigmoid(y))
        return parts

    def l2n(x):
        return x * lax.rsqrt(jnp.sum(x * x, axis=-1, keepdims=True) + EPS)

    for dst_ref, (c_ref, l_ref, w_ref), norm, scale in (
            (qn_ref, (qc_ref, ql_ref, cwq_ref), True, DN_DK ** -0.5),
            (kn_ref, (kc_ref, kl_ref, cwk_ref), True, 1.0),
            (vn_ref, (vc_ref, vl_ref, cwv_ref), False, 1.0)):
        yc, yl = conv_silu(c_ref, l_ref, w_ref)
        if norm:
            yc, yl = l2n(yc) * scale, l2n(yl) * scale
        dst_ref[0:CTX_LEN, :] = yc
        dst_ref[CTX_LEN:, :] = yl

    lane = lax.broadcasted_iota(jnp.int32, (1, LANE), 1)
    for ab_ref, r0, n in ((abc_ref, 0, CTX_LEN), (abl_ref, CTX_LEN, SEQ)):
        ab = ab_ref[...]
        g_all = -jnp.exp(alog_ref[...]) * _softplus(ab + dtb_ref[...])
        b_all = jax.nn.sigmoid(ab)
        for d in range(2):
            for k, (src, base) in enumerate(((g_all, 0), (b_all, 2 * DN_HEADS))):
                col = jnp.sum(jnp.where(lane == base + d * DN_HEADS + head, src, 0.0), axis=-1,
                              keepdims=True)
                gate_ref[2 * d + k, r0:r0 + n, :] = jnp.broadcast_to(col, (n, LANE))

    nt = (((1,), (1,)), ((), ()))

    def mm(x, y):
        return jnp.dot(x.astype(BF16), y.astype(BF16), preferred_element_type=F32)

    def block_body(it, carry):
        ii = lax.broadcasted_iota(jnp.int32, (cs, cs), 0)
        jj = lax.broadcasted_iota(jnp.int32, (cs, cs), 1)
        eye = ii == jj
        incl = (ii >= jj, ii <= jj)
        tri = tuple(jnp.where(m, 1.0, 0.0).astype(BF16) for m in incl)
        blocks = [it * DN_UNROLL + j for j in range(DN_UNROLL)]
        rows = [pl.ds(pl.multiple_of(c * cs, cs), cs) for c in blocks]
        q = [qn_ref[r, :] for r in rows]
        k = [kn_ref[r, :] for r in rows]
        v = [vn_ref[r, :] for r in rows]
        k16 = [x.astype(BF16) for x in k]
        kk = [lax.dot_general(x, x, nt, preferred_element_type=F32) for x in k16]
        qk = [lax.dot_general(x.astype(BF16), y, nt, preferred_element_type=F32) for x, y in zip(q, k16)]
        chains = [(j, d) for j in range(DN_UNROLL) for d in range(2)]
        g = [gate_ref[2 * d, rows[j], :] for j, d in chains]
        beta = [gate_ref[2 * d + 1, rows[j], :] for j, d in chains]
        g_hi = [x.astype(BF16) for x in g]
        g_lo = [(x - h.astype(F32)).astype(BF16) for x, h in zip(g, g_hi)]
        gc = [jnp.dot(tri[d], h, preferred_element_type=F32) + jnp.dot(tri[d], l, preferred_element_type=F32)
              for (j, d), h, l in zip(chains, g_hi, g_lo)]
        gc_last = [x[cs - 1:cs, :] if d == 0 else x[0:1, :] for (j, d), x in zip(chains, gc)]
        decay, a_mat = [], []
        for (j, d), x, bt in zip(chains, gc, beta):
            gc2 = jnp.concatenate([x] * (cs // LANE), axis=1)
            gc_row = jnp.sum(jnp.where(eye, gc2, 0.0), axis=0, keepdims=True)
            dec = jnp.where(incl[d], jnp.exp(jnp.where(incl[d], gc2 - gc_row, 0.0)), 0.0)
            decay.append(dec)
            a_mat.append(jnp.where(eye, 0.0, kk[j] * jnp.concatenate([bt] * (cs // LANE), axis=1) * dec))
        same = (ii >> DN_BASE_LOG2) == (jj >> DN_BASE_LOG2)
        t_mat = [jnp.where(eye, 1.0, 0.0) for _ in chains]
        p_mat = [jnp.where(same, -a, 0.0) for a in a_mat]
        for lvl in range(DN_BASE_LOG2 - 1):
            out = [mm(p, jnp.concatenate([t, p], axis=1)) for t, p in zip(t_mat, p_mat)]
            t_mat = [t + o[:, 0:cs] for t, o in zip(t_mat, out)]
            p_mat = [o[:, cs:] for o in out]
        t_mat = [t + mm(p, t) for t, p in zip(t_mat, p_mat)]
        for s in range(DN_BASE_LOG2, cs.bit_length() - 1):
            e_mask = jnp.logical_and((ii >> (s + 1)) == (jj >> (s + 1)), (ii >> s) != (jj >> s))
            et = [mm(jnp.where(e_mask, a, 0.0), t) for a, t in zip(a_mat, t_mat)]
            t_mat = [t - mm(t, e) for t, e in zip(t_mat, et)]
        e_gc = [jnp.exp(x) for x in gc]
        uw = [mm(t, jnp.concatenate([v[j] * bt, k[j] * bt * e], axis=1))
              for (j, d), t, bt, e in zip(chains, t_mat, beta, e_gc)]
        for n, (j, d) in enumerate(chains):
            idx = d * DN_NCH + blocks[j]
            u_ref[d, rows[j], :] = uw[n][:, 0:LANE]
            wq_ref[idx, 0:cs, :] = uw[n][:, LANE:].astype(BF16)
            wq_ref[idx, cs:, :] = (q[j] * e_gc[n]).astype(BF16)
            ik_ref[idx, 0:cs, :] = (qk[j] * decay[n]).astype(BF16)
            ik_ref[idx, cs:, :] = (k[j] * jnp.exp(gc_last[n] - gc[n])).T.astype(BF16)
            gl_ref[idx] = jnp.broadcast_to(jnp.exp(gc_last[n]), (HALO, LANE))
        return carry

    lax.fori_loop(0, DN_NCH // DN_UNROLL, block_body, 0)

    def scan_body(n, states):
        rev_c = jnp.where(n < DN_CTX_CH, DN_CTX_CH - 1 - n, DN_NCH + DN_CTX_CH - 1 - n)
        blk = (n, rev_c)
        idx = [d * DN_NCH + blk[d] for d in range(2)]
        rows = [pl.ds(pl.multiple_of(blk[d] * cs, cs), cs) for d in range(2)]
        ws_qs = [jnp.dot(wq_ref[idx[d]], states[d].astype(BF16), preferred_element_type=F32)
                 for d in range(2)]
        v_new = [u_ref[d, rows[d], :] - ws_qs[d][0:cs] for d in range(2)]
        r = [jnp.dot(ik_ref[idx[d]], v_new[d].astype(BF16), preferred_element_type=F32) for d in range(2)]
        for d in range(2):
            o_ref[d, rows[d], :] = ws_qs[d][cs:] + r[d][0:cs]
        return tuple(states[d] * gl_ref[idx[d]][0:1, :] + r[d][cs:] for d in range(2))

    s0 = jnp.zeros((DN_DK, DN_DV), F32)
    lax.fori_loop(0, DN_NCH, scan_body, (s0, s0))

    for y_ref, z_ref, r0, n in ((yc_ref, zc_ref, 0, CTX_LEN), (yl_ref, zl_ref, CTX_LEN, SEQ)):
        o = o_ref[0, r0:r0 + n, :] + o_ref[1, r0:r0 + n, :]
        z = z_ref[...]
        y = o * lax.rsqrt(jnp.mean(o * o, axis=-1, keepdims=True) + EPS) * nw_ref[...]
        y_ref[...] = y * (z * jax.nn.sigmoid(z))


def _deltanet(p_all, conv_w, a_log, dt_bias, norm_w, nb):
    ctx_blk0 = nb * SEQ // CTX_LEN
    qcol = OFF_DN // LANE
    zcol = OFF_DZ // LANE
    abcol = OFF_DA // LANE
    lat = lambda col: pl.BlockSpec((SEQ, LANE), lambda b, h: (b, col + h))
    ctx = lambda col: pl.BlockSpec((CTX_LEN, LANE), lambda b, h: (ctx_blk0 + b, col + h))
    cw = lambda k: pl.BlockSpec((3, LANE), lambda b, h: (0, k * DN_HEADS + h))
    small = pl.BlockSpec((1, LANE), lambda b, h: (0, 0))
    pad = lambda v: jnp.pad(v.reshape(1, -1), ((0, 0), (0, LANE - v.size)))
    n_idx = 2 * DN_NCH
    return pl.pallas_call(
        _deltanet_kernel,
        grid=(nb, DN_HEADS),
        in_specs=[ctx(qcol), ctx(qcol + DN_HEADS), ctx(qcol + 2 * DN_HEADS), ctx(zcol),
                  pl.BlockSpec((CTX_LEN, LANE), lambda b, h: (ctx_blk0 + b, abcol)),
                  lat(qcol), lat(qcol + DN_HEADS), lat(qcol + 2 * DN_HEADS), lat(zcol),
                  pl.BlockSpec((SEQ, LANE), lambda b, h: (b, abcol)),
                  cw(0), cw(1), cw(2), small, small, small],
        out_specs=[pl.BlockSpec((SEQ, LANE), lambda b, h: (b, h)),
                   pl.BlockSpec((CTX_LEN, LANE), lambda b, h: (b, h))],
        out_shape=[jax.ShapeDtypeStruct((nb * SEQ, BR_W), F32),
                   jax.ShapeDtypeStruct((nb * CTX_LEN, BR_W), F32)],
        scratch_shapes=[
            pltpu.VMEM((DN_LAT0 + SEQ + HALO, LANE), F32),
            pltpu.VMEM((DN_ROWS, LANE), F32),
            pltpu.VMEM((DN_ROWS, LANE), F32),
            pltpu.VMEM((DN_ROWS, LANE), F32),
            pltpu.VMEM((4, DN_ROWS, LANE), F32),
            pltpu.VMEM((2, DN_ROWS, LANE), F32),
            pltpu.VMEM((n_idx, 2 * DN_BLK, LANE), BF16),
            pltpu.VMEM((n_idx, DN_BLK + DN_DK, DN_BLK), BF16),
            pltpu.VMEM((n_idx, HALO, LANE), F32),
            pltpu.VMEM((2, DN_ROWS, LANE), F32),
        ],
        compiler_params=_params(2),
        name="deltanet",
    )(*([p_all] * 10), conv_w, conv_w, conv_w, pad(a_log), pad(dt_bias), norm_w.reshape(1, LANE))


def kernel(x, c, ctx, c_ctx, w_ada, b_ada, norm1, norm2, w_in, attn_qk_gain, attn_lambda, attn_subln,
           hy_conv, hy_w1, hy_b1, hy_freq, hy_w2, hy_b2, hy_w3, hy_bias, dn_conv, dn_a_log,
           dn_dt_bias, dn_norm, w_gate, w_branch, w_o, w_mlp1, w_mlp2):
    x_all = jnp.concatenate([x.reshape(N_LAT, D_MODEL), ctx.reshape(N_CTX, D_MODEL)], axis=0)
    cond = jax.nn.silu(jnp.concatenate([c, c_ctx[None, :]], axis=0))
    rope_tabs = _rope_tables()
    dft_lat = _hyena_fold_tables(SEQ)
    dft_ctx = _hyena_fold_tables(CTX_LEN)
    fn_lat = tuple(t.astype(BF16) for t in _dft_tables(SEQ, SEQ))
    fn_ctx = tuple(t.astype(BF16) for t in _dft_tables(CTX_LEN, CTX_LEN))
    fn_group = _fnet_group_tables()

    for li in range(DEPTH):
        need_ctx = li < DEPTH - 1
        lambda_init = 0.8 - 0.6 * math.exp(-0.3 * li)
        mod = (cond @ w_ada[li] + b_ada[li]).reshape(BATCH + 1, 6, D_MODEL)
        mod = jnp.pad(mod, ((0, 0), (0, MOD_ROWS - 6), (0, 0)))
        w_in_pad = jnp.pad(w_in[li], ((0, 0), (0, N_IN_PAD - N_IN))).astype(BF16)

        p_all = _proj_in(x_all, mod, norm1[li][None, :], w_in_pad)

        q_gain = jnp.tile(attn_qk_gain[li, 0], 2)[None, :]
        k_gain = jnp.tile(attn_qk_gain[li, 1], 2)[None, :]
        subln = attn_subln[li][None, :]
        ya = _attention(p_all, attn_lambda[li], q_gain, k_gain, subln, rope_tabs, lambda_init, True, BATCH)

        hy_filt = (hy_w1[li], hy_b1[li], hy_freq[li], hy_w2[li], hy_b2[li], hy_w3[li])
        yb = _hyena_fold(p_all, hy_conv[li], _hyena_filter_spectra(SEQ, *hy_filt), hy_bias[li], dft_lat,
                         SEQ, 0, BATCH)
        yf = _fnet(p_all, fn_group, fn_lat, SEQ, 0, BATCH)
        yd, yd_c = _deltanet(p_all, dn_conv[li], dn_a_log[li], dn_dt_bias[li], dn_norm[li], BATCH)
        merge_w = (w_gate[li].astype(BF16), w_branch[li].astype(BF16), w_o[li].astype(BF16))
        x_all = _merge(x_all, mod, norm1[li][None, :], [ya, yb, yf, yd], *merge_w, 0, N_LAT)
        n_rows = N_LAT
        if need_ctx:
            ya_c = _attention(p_all, attn_lambda[li], q_gain, k_gain, subln, None, lambda_init, False,
                              BATCH)
            yb_c = _hyena_fold(p_all, hy_conv[li], _hyena_filter_spectra(CTX_LEN, *hy_filt), hy_bias[li],
                               dft_ctx, CTX_LEN, N_LAT // CTX_LEN, BATCH)
            yf_c = _fnet(p_all, fn_group, fn_ctx, CTX_LEN, N_LAT // CTX_LEN, BATCH)
            x_all = _merge(x_all, mod, norm1[li][None, :], [ya_c, yb_c, yf_c, yd_c],
                           *merge_w, N_LAT // ROW_TILE, N_CTX)
            n_rows = N_ALL
        x_all = _mlp(x_all, mod, norm2[li][None, :], w_mlp1[li].astype(BF16),
                     w_mlp2[li].astype(BF16), n_rows)
    return x_all[:N_LAT].reshape(BATCH, SEQ, D_MODEL)
```

```python
import functools
import math

import jax
import jax.numpy as jnp
from jax import lax
from jax.experimental import pallas as pl
from jax.experimental.pallas import tpu as pltpu

F32 = jnp.float32
BF16 = jnp.bfloat16

D_MODEL = 1024
BATCH = 16
SEQ = 2048
DEPTH = 4
GRID_W = 64
CTX_LEN = 256
N_BRANCH = 4
BR_W = 512
A_HEADS = 4
A_DQK = 64
A_DV = 2 * A_DQK
ROPE_BASE = 10000.0
HY_W = BR_W
HY_ORDER = 2
HY_SHORT = 3
HY_BANDS = 16
HY_HID = 64
HY_MIN_DECAY = math.log(1e-2) / 1.5
HY_MAX_DECAY = math.log(1e-2) / 0.3
FN_W = BR_W
FN_GROUPS = 4
DN_HEADS = 4
DN_DK = 128
DN_DV = 128
DN_CHUNK = 64
D_FF = 4 * D_MODEL
EPS = 1e-6

OFF_AQ = 0
OFF_AK = OFF_AQ + A_HEADS * 2 * A_DQK
OFF_AV = OFF_AK + A_HEADS * 2 * A_DQK
OFF_HY = OFF_AV + A_HEADS * A_DV
OFF_FN = OFF_HY + (HY_ORDER + 1) * HY_W
OFF_DN = OFF_FN + FN_W
OFF_DZ = OFF_DN + DN_HEADS * (2 * DN_DK + DN_DV)
OFF_DA = OFF_DZ + DN_HEADS * DN_DV
OFF_DB = OFF_DA + 2 * DN_HEADS
N_IN = OFF_DB + 2 * DN_HEADS

LANE = 128
N_IN_PAD = 45 * LANE
N_LAT = BATCH * SEQ
N_CTX = BATCH * CTX_LEN
N_ALL = N_LAT + N_CTX
MOD_ROWS = 8
VMEM_LIMIT = 56 * 1024 * 1024

ROW_TILE = 512
ATT_QT = 512
ATT_ROWS = 256


def _params(n_axes):
    return pltpu.CompilerParams(dimension_semantics=("arbitrary",) * n_axes,
                                vmem_limit_bytes=VMEM_LIMIT)


def _mod_row(i):
    return jnp.minimum(i // (SEQ // ROW_TILE), BATCH)


def _ada_norm(x, nw, shift, scale):
    ms = jnp.mean(x * x, axis=-1, keepdims=True)
    return (x * lax.rsqrt(ms + EPS) * nw) * (1.0 + scale) + shift


def _proj_in_kernel(x_ref, mod_ref, nw_ref, w_ref, p_ref):
    h = _ada_norm(x_ref[...], nw_ref[...], mod_ref[0, 0:1, :], mod_ref[0, 1:2, :])
    p_ref[...] = jnp.dot(h.astype(BF16), w_ref[...], preferred_element_type=F32)


def _proj_in(x_all, mod, nw, w_in_pad):
    n_tiles = N_ALL // ROW_TILE
    return pl.pallas_call(
        _proj_in_kernel,
        grid=(n_tiles,),
        in_specs=[
            pl.BlockSpec((ROW_TILE, D_MODEL), lambda i: (i, 0)),
            pl.BlockSpec((1, MOD_ROWS, D_MODEL), lambda i: (_mod_row(i), 0, 0)),
            pl.BlockSpec((1, D_MODEL), lambda i: (0, 0)),
            _resident((D_MODEL, N_IN_PAD), lambda i: (0, 0)),
        ],
        out_specs=pl.BlockSpec((ROW_TILE, N_IN_PAD), lambda i: (i, 0)),
        out_shape=jax.ShapeDtypeStruct((N_ALL, N_IN_PAD), F32),
        compiler_params=_params(1),
        name="proj_in",
    )(x_all, mod, nw, w_in_pad)


def _merge_kernel(x_ref, mod_ref, nw_ref, ya_ref, yb_ref, yf_ref, yd_ref, wg_ref, wb_ref, wo_ref,
                  o_ref):
    x = x_ref[...]
    h = _ada_norm(x, nw_ref[...], mod_ref[0, 0:1, :], mod_ref[0, 1:2, :]).astype(BF16)
    acc = None
    for n, y_ref in enumerate((ya_ref, yb_ref, yf_ref, yd_ref)):
        gate = jax.nn.sigmoid(jnp.dot(h, wg_ref[n], preferred_element_type=F32))
        br = jnp.dot(y_ref[...].astype(BF16), wb_ref[n], preferred_element_type=F32)
        acc = gate * br if acc is None else acc + gate * br
    out = jnp.dot(acc.astype(BF16), wo_ref[...], preferred_element_type=F32)
    o_ref[...] = x + mod_ref[0, 2:3, :] * out


def _merge(x_all, mod, nw, ys, wg, wb, wo, row_blk0, n_rows):
    xrow = lambda i: (row_blk0 + i, 0)
    yrow = lambda i: (i, 0)
    const2 = lambda i: (0, 0)
    const3 = lambda i: (0, 0, 0)
    return pl.pallas_call(
        _merge_kernel,
        grid=(n_rows // ROW_TILE,),
        in_specs=[
            pl.BlockSpec((ROW_TILE, D_MODEL), xrow),
            pl.BlockSpec((1, MOD_ROWS, D_MODEL), lambda i: (_mod_row(row_blk0 + i), 0, 0)),
            pl.BlockSpec((1, D_MODEL), const2),
        ] + [pl.BlockSpec((ROW_TILE, BR_W), yrow)] * N_BRANCH + [
            _resident((N_BRANCH, D_MODEL, D_MODEL), const3),
            _resident((N_BRANCH, BR_W, D_MODEL), const3),
            _resident((D_MODEL, D_MODEL), const2),
        ],
        out_specs=pl.BlockSpec((ROW_TILE, D_MODEL), xrow),
        out_shape=jax.ShapeDtypeStruct((N_ALL, D_MODEL), F32),
        input_output_aliases={0: 0},
        compiler_params=_params(1),
        name="merge",
    )(x_all, mod, nw, *ys, wg, wb, wo)


def _mlp_kernel(x_ref, mod_ref, nw_ref, w1_ref, w2_ref, o_ref):
    x = x_ref[...]
    h = _ada_norm(x, nw_ref[...], mod_ref[0, 3:4, :], mod_ref[0, 4:5, :]).astype(BF16)
    u = jnp.maximum(jnp.dot(h, w1_ref[...], preferred_element_type=F32), 0.0)
    out = jnp.dot((u * u).astype(BF16), w2_ref[...], preferred_element_type=F32)
    o_ref[...] = x + mod_ref[0, 5:6, :] * out


def _mlp(x_all, mod, nw, w1, w2, n_rows):
    row = lambda i: (i, 0)
    const2 = lambda i: (0, 0)
    return pl.pallas_call(
        _mlp_kernel,
        grid=(n_rows // ROW_TILE,),
        in_specs=[
            pl.BlockSpec((ROW_TILE, D_MODEL), row),
            pl.BlockSpec((1, MOD_ROWS, D_MODEL), lambda i: (_mod_row(i), 0, 0)),
            pl.BlockSpec((1, D_MODEL), const2),
            _resident((D_MODEL, D_FF), const2),
            _resident((D_FF, D_MODEL), const2),
        ],
        out_specs=pl.BlockSpec((ROW_TILE, D_MODEL), row),
        out_shape=jax.ShapeDtypeStruct((N_ALL, D_MODEL), F32),
        input_output_aliases={0: 0},
        compiler_params=_params(1),
        name="mlp",
    )(x_all, mod, nw, w1, w2)


def _qk_norm(x, gain):
    lane = lax.broadcasted_iota(jnp.int32, x.shape, 1)
    first = lane < A_DQK
    sq = x * x
    s_all = jnp.sum(sq, axis=-1, keepdims=True)
    s_first = jnp.sum(jnp.where(first, sq, 0.0), axis=-1, keepdims=True)
    ms = jnp.where(first, s_first, s_all - s_first) * (1.0 / A_DQK)
    return x * lax.rsqrt(ms + EPS) * gain


def _rope(x, cos, sin_lo, sin_hi):
    up = pltpu.roll(x, LANE - A_DQK // 4, 1)
    down = pltpu.roll(x, A_DQK // 4, 1)
    return x * cos + up * sin_lo + down * sin_hi


def _attn_kernel(lam_ref, qg_ref, kg_ref, sub_ref, q_ref, *rest, with_latent, lambda_init):
    if with_latent:
        (kc_ref, vc_ref, kl_ref, vl_ref, cq_ref, sq_lo_ref, sq_hi_ref,
         ck_ref, sk_lo_ref, sk_hi_ref, o_ref, kn_ref, vn_ref) = rest
    else:
        kc_ref, vc_ref, o_ref, kn_ref, vn_ref = rest

    @pl.when(pl.program_id(2) == 0)
    def _prep_keys():
        vn_ref[:, LANE:] = jnp.ones((vn_ref.shape[0], LANE), BF16)
        kn_ref[0:CTX_LEN, :] = _qk_norm(kc_ref[...], kg_ref[...]).astype(BF16)
        vn_ref[0:CTX_LEN, 0:LANE] = vc_ref[...].astype(BF16)
        if with_latent:
            kl = _qk_norm(kl_ref[...], kg_ref[...])
            kl = _rope(kl, ck_ref[...], sk_lo_ref[...], sk_hi_ref[...])
            kn_ref[CTX_LEN:, :] = kl.astype(BF16)
            vn_ref[CTX_LEN:, 0:LANE] = vl_ref[...].astype(BF16)

    q = _qk_norm(q_ref[...], qg_ref[...])
    if with_latent:
        q = _rope(q, cq_ref[...], sq_lo_ref[...], sq_hi_ref[...])
    q = q * (A_DQK ** -0.5 * math.log2(math.e))
    lane = lax.broadcasted_iota(jnp.int32, q.shape, 1)
    first = lane < A_DQK
    kn = kn_ref[...]
    vn = vn_ref[...]

    n_grp = max(q.shape[0] // ATT_ROWS, 1)
    rows = q.shape[0] // n_grp
    qs = [jnp.where(first if c == 0 else jnp.logical_not(first), q, 0.0)[g * rows:(g + 1) * rows]
          for g in range(n_grp) for c in range(2)]
    s = [lax.dot_general(x.astype(BF16), kn, (((1,), (1,)), ((), ())), preferred_element_type=F32)
         for x in qs]
    e = [jnp.exp2((x - jnp.max(x, axis=-1, keepdims=True)).astype(BF16)) for x in s]
    ov = [jnp.dot(x, vn, preferred_element_type=F32) for x in e]
    oc = [x[:, 0:LANE] / x[:, LANE:LANE + 1] for x in ov]

    lf = lam_ref[...]
    lam = (jnp.exp(jnp.sum(lf[0:1] * lf[1:2], axis=-1, keepdims=True))
           - jnp.exp(jnp.sum(lf[2:3] * lf[3:4], axis=-1, keepdims=True)) + lambda_init)
    o = jnp.concatenate([oc[2 * g] - lam * oc[2 * g + 1] for g in range(n_grp)], axis=0)
    ms = jnp.mean(o * o, axis=-1, keepdims=True)
    o_ref[...] = (o * lax.rsqrt(ms + EPS) * sub_ref[...]) * (1.0 - lambda_init)


def _attention(p, lam_vec, q_gain, k_gain, subln, rope_tabs, lambda_init, with_latent, nb):
    kcol = OFF_AK // LANE
    vcol = OFF_AV // LANE
    ctx_blk0 = nb * SEQ // CTX_LEN
    small = lambda b, h, i: (0, 0)
    if with_latent:
        qt, nq = ATT_QT, SEQ // ATT_QT
        q_spec = pl.BlockSpec((qt, LANE), lambda b, h, i: (b * nq + i, h))
        n_keys = CTX_LEN + SEQ
    else:
        qt, nq = CTX_LEN, 1
        q_spec = pl.BlockSpec((qt, LANE), lambda b, h, i: (ctx_blk0 + b, h))
        n_keys = CTX_LEN
    in_specs = [
        pl.BlockSpec((4, A_DQK), small),
        pl.BlockSpec((1, LANE), small),
        pl.BlockSpec((1, LANE), small),
        pl.BlockSpec((1, LANE), small),
        q_spec,
        pl.BlockSpec((CTX_LEN, LANE), lambda b, h, i: (ctx_blk0 + b, kcol + h)),
        pl.BlockSpec((CTX_LEN, LANE), lambda b, h, i: (ctx_blk0 + b, vcol + h)),
    ]
    args = [lam_vec, q_gain, k_gain, subln, p, p, p]
    if with_latent:
        in_specs += [
            pl.BlockSpec((SEQ, LANE), lambda b, h, i: (b, kcol + h)),
            pl.BlockSpec((SEQ, LANE), lambda b, h, i: (b, vcol + h)),
        ] + [pl.BlockSpec((qt, LANE), lambda b, h, i: (i, 0))] * 3 + [
            pl.BlockSpec((SEQ, LANE), small)] * 3
        args += [p, p] + list(rope_tabs) + list(rope_tabs)
    n_rows = nb * (SEQ if with_latent else CTX_LEN)
    return pl.pallas_call(
        functools.partial(_attn_kernel, with_latent=with_latent, lambda_init=lambda_init),
        grid=(nb, A_HEADS, nq),
        in_specs=in_specs,
        out_specs=pl.BlockSpec((qt, LANE), lambda b, h, i: (b * nq + i, h)),
        out_shape=jax.ShapeDtypeStruct((n_rows, A_HEADS * A_DV), F32),
        scratch_shapes=[pltpu.VMEM((n_keys, LANE), BF16), pltpu.VMEM((n_keys, 2 * LANE), BF16)],
        compiler_params=_params(3),
        name="diff_attn_latent" if with_latent else "diff_attn_ctx",
    )(*args)


def _rope_tables():
    t = jnp.arange(SEQ, dtype=jnp.int32)
    pos = jnp.stack([(t // GRID_W).astype(F32), (t % GRID_W).astype(F32)], axis=1)
    n_freq = A_DQK // 4
    inv = ROPE_BASE ** (-jnp.arange(n_freq, dtype=F32) / n_freq)
    ang = pos[:, :, None] * inv
    cos = jnp.cos(ang)[:, None, :, None, :]
    sin = jnp.sin(ang)[:, None, :, None, :]
    shape = (SEQ, 2, 2, 2, n_freq)
    half = jnp.arange(2).reshape(1, 1, 1, 2, 1)
    cos_t = jnp.broadcast_to(cos, shape).reshape(SEQ, LANE)
    sin_lo = jnp.where(half == 0, -jnp.broadcast_to(sin, shape), 0.0).reshape(SEQ, LANE)
    sin_hi = jnp.where(half == 1, jnp.broadcast_to(sin, shape), 0.0).reshape(SEQ, LANE)
    return cos_t, sin_lo, sin_hi


def _hyena_filter_spectra(length, w1, b1, freq, w2, b2, w3):
    t = jnp.linspace(0.0, 1.0, length, dtype=F32)[:, None]
    w = 2.0 * math.pi * jnp.arange(length, dtype=F32)[:, None] / length
    bands = jnp.linspace(1e-4, HY_BANDS - 1, HY_BANDS, dtype=F32)[None]
    feats = jnp.concatenate([t, jnp.cos(bands * w), -jnp.sin(bands * w)], axis=-1)
    hdn = jnp.sin(freq[0] * (feats @ w1 + b1))
    hdn = jnp.sin(freq[1] * (hdn @ w2 + b2))
    h = (hdn @ w3).reshape(length, HY_ORDER, 2, HY_W)
    deltas = jnp.abs(jnp.linspace(HY_MIN_DECAY, HY_MAX_DECAY, HY_W, dtype=F32))
    h = h * jnp.exp(-t[:, :, None, None] * deltas)
    past = h[:, :, 0].reshape(length, HY_ORDER * HY_W)
    fut = h[:, :, 1].at[0].set(0.0).reshape(length, HY_ORDER * HY_W)
    norm = jnp.sum(jnp.abs(past) + jnp.abs(fut), axis=0, keepdims=True)
    return _tap_spectrum((past + fut) / norm, (past - fut) / norm, length)


def _tap_spectrum_kernel(sum_ref, dif_ref, c_ref, s_ref, kr_ref, ki_ref, kn_ref):
    a = sum_ref[...]
    kr_ref[...] = jnp.dot(c_ref[...], a.astype(BF16), preferred_element_type=F32)
    ki_ref[...] = -jnp.dot(s_ref[...], dif_ref[...].astype(BF16), preferred_element_type=F32)
    t = lax.broadcasted_iota(jnp.int32, (a.shape[0], 1), 0)
    nyq = jnp.sum(a * (1 - 2 * (t & 1)).astype(F32), axis=0, keepdims=True)
    kn_ref[...] = jnp.broadcast_to(nyq, kn_ref.shape)


def _tap_spectrum(tap_sum, tap_dif, length):
    n = tap_sum.shape[1]
    cos, sin = (t.astype(BF16) for t in _dft_tables(length, 2 * length))
    col = pl.BlockSpec((length, HY_CB), lambda c: (0, c))
    table = _resident((length, length), lambda c: (0, 0))
    kr, ki, kn = pl.pallas_call(
        _tap_spectrum_kernel,
        grid=(n // HY_CB,),
        in_specs=[col, col, table, table],
        out_specs=[col, col, pl.BlockSpec((HALO, HY_CB), lambda c: (0, c))],
        out_shape=[jax.ShapeDtypeStruct((length, n), F32)] * 2 + [jax.ShapeDtypeStruct((HALO, n), F32)],
        compiler_params=_params(1),
        name="tap_spectrum_%d" % length,
    )(tap_sum, tap_dif, cos, sin)
    shape = (length + 1, HY_ORDER, HY_W)
    return (jnp.concatenate([kr, kn[0:1]], axis=0).reshape(shape),
            jnp.concatenate([ki, jnp.zeros((1, n), F32)], axis=0).reshape(shape))


def _dft_tables(n, period):
    idx = jnp.arange(n, dtype=jnp.int32)
    ang = ((idx[:, None] * idx[None, :]) % period).astype(F32) * (2.0 * math.pi / period)
    return jnp.cos(ang), jnp.sin(ang)


def _resident(shape, index_map):
    return pl.BlockSpec(shape, index_map, pipeline_mode=pl.Buffered(1))


HY_CB = 256
HY_ROWS = 512
HALO = 8


def _hyena_fold_kernel(*refs, length, rows):
    tiles = HY_CB // LANE
    x1_refs, x2_refs, v_refs = refs[0:tiles], refs[tiles:2 * tiles], refs[2 * tiles:3 * tiles]
    (cw1_ref, cw2_ref, cwv_ref, kt_ref, km_ref, bias_ref, cfe_ref, sfe_ref, cfo_ref, sfo_ref, cfot_ref,
     sfot_ref, oe_ref, oo_ref, pe_ref, po_ref, ze_ref, zo_ref, zeb_ref, zob_ref, ya_ref, yb_ref, yc_ref,
     yd_ref) = refs[3 * tiles:]
    half = length // 2
    starts = range(0, half, rows)
    zeros = jnp.zeros((HALO, HY_CB), F32)
    for ref in (pe_ref, po_ref):
        ref[0:HALO, :] = zeros
        ref[HALO + half:, :] = zeros

    def load_split(part_refs):
        for t, ref in enumerate(part_refs):
            pe_ref[HALO:HALO + half, t * LANE:(t + 1) * LANE] = ref[pl.ds(0, half, stride=2), :]
            po_ref[HALO:HALO + half, t * LANE:(t + 1) * LANE] = ref[pl.ds(1, half, stride=2), :]

    def short_conv(w_ref, r0):
        w = w_ref[...]
        lo, hi = r0 + HALO, r0 + HALO + rows
        xe, xo = pe_ref[lo:hi, :], po_ref[lo:hi, :]
        even = po_ref[lo - 1:hi - 1, :] * w[0:1] + xe * w[1:2] + xo * w[2:3]
        odd = xe * w[0:1] + xo * w[1:2] + pe_ref[lo + 1:hi + 1, :] * w[2:3]
        return even, odd

    def put_z(r0, even, odd):
        ze_ref[r0:r0 + rows, :] = even
        zo_ref[r0:r0 + rows, :] = odd
        zeb_ref[r0:r0 + rows, :] = even.astype(BF16)
        zob_ref[r0:r0 + rows, :] = odd.astype(BF16)

    j = lax.broadcasted_iota(jnp.int32, (rows, 1), 0)
    alt = (1 - 2 * (j & 1)).astype(F32)

    def mm(t_ref, r0, x):
        return jnp.dot(t_ref[r0:r0 + rows, :], x, preferred_element_type=F32)

    load_split(v_refs)
    for r0 in starts:
        put_z(r0, *short_conv(cwv_ref, r0))

    for n, (gate_refs, gw_ref) in enumerate(((x1_refs, cw1_ref), (x2_refs, cw2_ref))):
        load_split(gate_refs)
        zr_mid = jnp.zeros((1, HY_CB), F32)
        zi_mid = jnp.zeros((1, HY_CB), F32)
        for r0 in starts:
            zeb, zob = zeb_ref[...], zob_ref[...]
            a, b = mm(cfe_ref, r0, zeb), mm(cfo_ref, r0, zob)
            a_s, b_s = mm(sfe_ref, r0, zeb), mm(sfo_ref, r0, zob)
            blk = slice(r0, r0 + rows)
            kr_lo, ki_lo, kr_hi, ki_hi = (kt_ref[n, i, blk, :] for i in range(4))
            zr, zi = a + b, a_s + b_s
            yr_lo, yi_lo = zr * kr_lo + zi * ki_lo, zr * ki_lo - zi * kr_lo
            zr, zi = a - b, b_s - a_s
            yr_hi, yi_hi = zr * kr_hi + zi * ki_hi, zr * ki_hi - zi * kr_hi
            ya_ref[blk, :] = (yr_lo + yr_hi).astype(BF16)
            yb_ref[blk, :] = (yi_lo - yi_hi).astype(BF16)
            yc_ref[blk, :] = (yr_lo - yr_hi).astype(BF16)
            yd_ref[blk, :] = (yi_lo + yi_hi).astype(BF16)
            zr_mid = zr_mid + jnp.sum(ze_ref[blk, :] * alt, axis=0, keepdims=True)
            zi_mid = zi_mid + jnp.sum(zo_ref[blk, :] * alt, axis=0, keepdims=True)
        kr_mid, ki_mid = km_ref[2 * n:2 * n + 1, :], km_ref[2 * n + 1:2 * n + 2, :]
        yr_mid = zr_mid * kr_mid + zi_mid * ki_mid
        yi_mid = zr_mid * ki_mid - zi_mid * kr_mid
        bias = bias_ref[n:n + 1, :]
        for r0 in starts:
            blk = slice(r0, r0 + rows)
            y_even = mm(cfe_ref, r0, ya_ref[...]) - mm(sfe_ref, r0, yb_ref[...]) + alt * yr_mid
            y_odd = mm(cfot_ref, r0, yc_ref[...]) - mm(sfot_ref, r0, yd_ref[...]) - alt * yi_mid
            g_even, g_odd = short_conv(gw_ref, r0)
            put_z(r0, g_even * (y_even + bias * ze_ref[blk, :]), g_odd * (y_odd + bias * zo_ref[blk, :]))

    oe_ref[...] = ze_ref[...]
    oo_ref[...] = zo_ref[...]


def _hyena_fold_tables(length):
    half = length // 2
    f = jnp.arange(half, dtype=jnp.int32)[:, None]
    j = jnp.arange(half, dtype=jnp.int32)[None, :]
    unit = math.pi / length
    even = ((f * 2 * j) % (2 * length)).astype(F32) * unit
    odd = ((f * (2 * j + 1)) % (2 * length)).astype(F32) * unit
    cfo, sfo = jnp.cos(odd), jnp.sin(odd)
    return tuple(t.astype(BF16) for t in (jnp.cos(even), jnp.sin(even), cfo, sfo, cfo.T, sfo.T))


def _hyena_fold(p_all, conv_w, kf, bias, tables, length, row_blk0, nb):
    half = length // 2
    k_re, k_im = kf
    wt = jnp.full((half, 1, 1), 1.0 / length, F32).at[0].set(0.5 / length)
    kt = jnp.stack([k_re[:half] * wt, k_im[:half] * wt,
                    k_re[length:half:-1] * wt, k_im[length:half:-1] * wt], axis=0)
    kt = jnp.transpose(kt, (2, 0, 1, 3))
    km = jnp.stack([k_re[half], k_im[half]], axis=1).reshape(2 * HY_ORDER, HY_W) * (1.0 / length)
    per = HY_W // HY_CB
    tiles = HY_CB // LANE

    def part(k):
        col = (OFF_HY + k * HY_W) // LANE
        return [pl.BlockSpec((length, LANE), lambda c, b, o=col + t: (row_blk0 + b, o + c * tiles))
                for t in range(tiles)]

    cw = lambda k: _resident((HY_SHORT, HY_CB), lambda c, b: (0, k * per + c))
    table = _resident((half, half), lambda c, b: (0, 0))
    out_spec = pl.BlockSpec((half, HY_CB), lambda c, b: (b, c))
    out_shape = jax.ShapeDtypeStruct((nb * half, HY_W), F32)
    y_even, y_odd = pl.pallas_call(
        functools.partial(_hyena_fold_kernel, length=length, rows=min(half, HY_ROWS)),
        grid=(per, nb),
        in_specs=part(0) + part(1) + part(2) + [
            cw(0), cw(1), cw(2),
            _resident((HY_ORDER, 4, half, HY_CB), lambda c, b: (0, 0, 0, c)),
            _resident((2 * HY_ORDER, HY_CB), lambda c, b: (0, c)),
            _resident((HY_ORDER, HY_CB), lambda c, b: (0, c))] + [table] * 6,
        out_specs=[out_spec, out_spec],
        out_shape=[out_shape, out_shape],
        scratch_shapes=[pltpu.VMEM((half + 2 * HALO, HY_CB), F32)] * 2
        + [pltpu.VMEM((half, HY_CB), F32)] * 2 + [pltpu.VMEM((half, HY_CB), BF16)] * 6,
        compiler_params=_params(2),
        name="hyena_%d" % length,
    )(*([p_all] * (3 * tiles)), conv_w, conv_w, conv_w, kt, km, bias, *tables)
    return jnp.stack([y_even, y_odd], axis=1).reshape(nb * length, HY_W)


FN_CB = 256
FN_GW = FN_W // FN_GROUPS


def _fnet_kernel(*refs, scale, half):
    tiles = FN_CB // LANE
    u_refs = refs[0:tiles]
    cg_ref, sg_ref, ce_ref, se_ref, co_ref, so_ref, o_ref = refs[tiles:]

    def real_part(par, c_ref, s_ref):
        u = jnp.concatenate([r[pl.ds(par, half, stride=2), :] for r in u_refs], axis=1).astype(BF16)
        uc = jnp.dot(u, cg_ref[...], preferred_element_type=F32).astype(BF16)
        us = jnp.dot(u, sg_ref[...], preferred_element_type=F32).astype(BF16)
        return (jnp.dot(c_ref[...], uc, preferred_element_type=F32)
                - jnp.dot(s_ref[...], us, preferred_element_type=F32))

    a = real_part(0, ce_ref, se_ref)
    b = real_part(1, co_ref, so_ref)
    o_ref[0:half, :] = (a + b) * scale
    o_ref[half:, :] = (a - b) * scale


def _fnet_seq_tables(length):
    half = length // 2
    f = jnp.arange(half, dtype=jnp.int32)[:, None]
    j = jnp.arange(half, dtype=jnp.int32)[None, :]
    unit = 2.0 * math.pi / length
    even = ((f * 2 * j) % length).astype(F32) * unit
    odd = ((f * (2 * j + 1)) % length).astype(F32) * unit
    return tuple(t.astype(BF16) for t in (jnp.cos(even), jnp.sin(even), jnp.cos(odd), jnp.sin(odd)))


def _fnet(p_all, group_dft, seq_dft, length, row_blk0, nb):
    half = length // 2
    tiles = FN_CB // LANE
    col0 = OFF_FN // LANE
    small = lambda b, c: (0, 0)
    u_specs = [pl.BlockSpec((length, LANE), lambda b, c, t=t: (row_blk0 + b, col0 + c * tiles + t))
               for t in range(tiles)]
    return pl.pallas_call(
        functools.partial(_fnet_kernel, scale=(length * FN_GW) ** -0.5, half=half),
        grid=(nb, FN_W // FN_CB),
        in_specs=u_specs + [_resident((FN_CB, FN_CB), small)] * 2 + [_resident((half, half), small)] * 4,
        out_specs=pl.BlockSpec((length, FN_CB), lambda b, c: (b, c)),
        out_shape=jax.ShapeDtypeStruct((nb * length, FN_W), F32),
        compiler_params=_params(2),
        name="fnet_%d" % length,
    )(*([p_all] * tiles), *group_dft, *seq_dft)


def _fnet_group_tables():
    cg, sg = _dft_tables(FN_GW, FN_GW)
    eye = jnp.eye(FN_CB // FN_GW, dtype=F32)
    return jnp.kron(eye, cg).astype(BF16), jnp.kron(eye, sg).astype(BF16)


DN_ROWS = CTX_LEN + SEQ
DN_BLK = 256
DN_NCH = DN_ROWS // DN_BLK
DN_CTX_CH = CTX_LEN // DN_BLK
DN_UNROLL = 3
DN_LAT0 = CTX_LEN + 2 * HALO
DN_BASE_LOG2 = 3


def _softplus(x):
    return jnp.maximum(x, 0.0) + jnp.log1p(jnp.exp(-jnp.abs(x)))


def _deltanet_kernel(qc_ref, kc_ref, vc_ref, zc_ref, abc_ref, ql_ref, kl_ref, vl_ref, zl_ref, abl_ref,
                     cwq_ref, cwk_ref, cwv_ref, alog_ref, dtb_ref, nw_ref, yl_ref, yc_ref,
                     pad_ref, qn_ref, kn_ref, vn_ref, gall_ref, gate_ref, u_ref, wq_ref, ik_ref, gl_ref,
                     o_ref):
    head = pl.program_id(1)
    cs = DN_BLK

    zeros = jnp.zeros((HALO, LANE), F32)
    pad_ref[0:HALO, :] = zeros
    pad_ref[HALO + CTX_LEN:DN_LAT0, :] = zeros
    pad_ref[DN_LAT0 + SEQ:, :] = zeros

    def conv_silu(c_ref, l_ref, w_ref):
        pad_ref[HALO:HALO + CTX_LEN, :] = c_ref[...]
        pad_ref[DN_LAT0:DN_LAT0 + SEQ, :] = l_ref[...]
        w = w_ref[...]
        parts = []
        for start, n in ((HALO, CTX_LEN), (DN_LAT0, SEQ)):
            y = (pad_ref[start - 1:start - 1 + n, :] * w[0:1] + pad_ref[start:start + n, :] * w[1:2]
                 + pad_ref[start + 1:start + 1 + n, :] * w[2:3])
            parts.append(y * jax.nn.sigmoid(y))
        return parts

    def l2n(x):
        return x * lax.rsqrt(jnp.sum(x * x, axis=-1, keepdims=True) + EPS)

    for dst_ref, (c_ref, l_ref, w_ref), norm, scale in (
            (qn_ref, (qc_ref, ql_ref, cwq_ref), True, DN_DK ** -0.5),
            (kn_ref, (kc_ref, kl_ref, cwk_ref), True, 1.0),
            (vn_ref, (vc_ref, vl_ref, cwv_ref), False, 1.0)):
        yc, yl = conv_silu(c_ref, l_ref, w_ref)
        if norm:
            yc, yl = l2n(yc) * scale, l2n(yl) * scale
        dst_ref[0:CTX_LEN, :] = yc
        dst_ref[CTX_LEN:, :] = yl

    @pl.when(head == 0)
    def _all_head_gates():
        for ab_ref, r0, n in ((abc_ref, 0, CTX_LEN), (abl_ref, CTX_LEN, SEQ)):
            ab = ab_ref[...]
            gall_ref[0, r0:r0 + n, :] = -jnp.exp(alog_ref[...]) * _softplus(ab + dtb_ref[...])
            gall_ref[1, r0:r0 + n, :] = jax.nn.sigmoid(ab)

    lane = lax.broadcasted_iota(jnp.int32, (1, LANE), 1)
    for d in range(2):
        for k, base in enumerate((0, 2 * DN_HEADS)):
            col = jnp.sum(jnp.where(lane == base + d * DN_HEADS + head, gall_ref[k], 0.0), axis=-1,
                          keepdims=True)
            gate_ref[2 * d + k] = jnp.broadcast_to(col, (DN_ROWS, LANE))

    nt = (((1,), (1,)), ((), ()))

    def mm(x, y):
        return jnp.dot(x.astype(BF16), y.astype(BF16), preferred_element_type=F32)

    def block_body(it, carry):
        ii = lax.broadcasted_iota(jnp.int32, (cs, cs), 0)
        jj = lax.broadcasted_iota(jnp.int32, (cs, cs), 1)
        eye = ii == jj
        incl = (ii >= jj, ii <= jj)
        tri = tuple(jnp.where(m, 1.0, 0.0).astype(BF16) for m in incl)
        blocks = [it * DN_UNROLL + j for j in range(DN_UNROLL)]
        rows = [pl.ds(pl.multiple_of(c * cs, cs), cs) for c in blocks]
        q = [qn_ref[r, :] for r in rows]
        k = [kn_ref[r, :] for r in rows]
        v = [vn_ref[r, :] for r in rows]
        k16 = [x.astype(BF16) for x in k]
        kk = [lax.dot_general(x, x, nt, preferred_element_type=F32) for x in k16]
        qk = [lax.dot_general(x.astype(BF16), y, nt, preferred_element_type=F32) for x, y in zip(q, k16)]
        chains = [(j, d) for j in range(DN_UNROLL) for d in range(2)]
        g = [gate_ref[2 * d, rows[j], :] for j, d in chains]
        beta = [gate_ref[2 * d + 1, rows[j], :] for j, d in chains]
        g_hi = [x.astype(BF16) for x in g]
        g_lo = [(x - h.astype(F32)).astype(BF16) for x, h in zip(g, g_hi)]
        gc = [jnp.dot(tri[d], h, preferred_element_type=F32) + jnp.dot(tri[d], l, preferred_element_type=F32)
              for (j, d), h, l in zip(chains, g_hi, g_lo)]
        gc_last = [x[cs - 1:cs, :] if d == 0 else x[0:1, :] for (j, d), x in zip(chains, gc)]
        decay, a_mat = [], []
        for (j, d), x, bt in zip(chains, gc, beta):
            gc2 = jnp.concatenate([x] * (cs // LANE), axis=1)
            gc_row = jnp.sum(jnp.where(eye, gc2, 0.0), axis=0, keepdims=True)
            dec = jnp.where(incl[d], jnp.exp(jnp.where(incl[d], gc2 - gc_row, 0.0)), 0.0)
            decay.append(dec)
            a_mat.append(jnp.where(eye, 0.0, kk[j] * jnp.concatenate([bt] * (cs // LANE), axis=1) * dec))
        same = (ii >> DN_BASE_LOG2) == (jj >> DN_BASE_LOG2)
        t_mat = [jnp.where(eye, 1.0, 0.0) for _ in chains]
        p_mat = [jnp.where(same, -a, 0.0) for a in a_mat]
        for lvl in range(DN_BASE_LOG2 - 1):
            out = [mm(p, jnp.concatenate([t, p], axis=1)) for t, p in zip(t_mat, p_mat)]
            t_mat = [t + o[:, 0:cs] for t, o in zip(t_mat, out)]
            p_mat = [o[:, cs:] for o in out]
        t_mat = [t + mm(p, t) for t, p in zip(t_mat, p_mat)]
        for s in range(DN_BASE_LOG2, cs.bit_length() - 1):
            e_mask = jnp.logical_and((ii >> (s + 1)) == (jj >> (s + 1)), (ii >> s) != (jj >> s))
            et = [mm(jnp.where(e_mask, a, 0.0), t) for a, t in zip(a_mat, t_mat)]
            t_mat = [t - mm(t, e) for t, e in zip(t_mat, et)]
        e_gc = [jnp.exp(x) for x in gc]
        uw = [mm(t, jnp.concatenate([v[j] * bt, k[j] * bt * e], axis=1))
              for (j, d), t, bt, e in zip(chains, t_mat, beta, e_gc)]
        for n, (j, d) in enumerate(chains):
            idx = d * DN_NCH + blocks[j]
            u_ref[d, rows[j], :] = uw[n][:, 0:LANE]
            wq_ref[idx, 0:cs, :] = uw[n][:, LANE:].astype(BF16)
            wq_ref[idx, cs:, :] = (q[j] * e_gc[n]).astype(BF16)
            ik_ref[idx, 0:cs, :] = (qk[j] * decay[n]).astype(BF16)
            ik_ref[idx, cs:, :] = (k[j] * jnp.exp(gc_last[n] - gc[n])).T.astype(BF16)
            gl_ref[idx] = jnp.broadcast_to(jnp.exp(gc_last[n]), (HALO, LANE))
        return carry

    lax.fori_loop(0, DN_NCH // DN_UNROLL, block_body, 0)

    def scan_body(n, states):
        rev_c = jnp.where(n < DN_CTX_CH, DN_CTX_CH - 1 - n, DN_NCH + DN_CTX_CH - 1 - n)
        blk = (n, rev_c)
        idx = [d * DN_NCH + blk[d] for d in range(2)]
        rows = [pl.ds(pl.multiple_of(blk[d] * cs, cs), cs) for d in range(2)]
        ws_qs = [jnp.dot(wq_ref[idx[d]], states[d].astype(BF16), preferred_element_type=F32)
                 for d in range(2)]
        v_new = [u_ref[d, rows[d], :] - ws_qs[d][0:cs] for d in range(2)]
        r = [jnp.dot(ik_ref[idx[d]], v_new[d].astype(BF16), preferred_element_type=F32) for d in range(2)]
        for d in range(2):
            o_ref[d, rows[d], :] = ws_qs[d][cs:] + r[d][0:cs]
        return tuple(states[d] * gl_ref[idx[d]][0:1, :] + r[d][cs:] for d in range(2))

    s0 = jnp.zeros((DN_DK, DN_DV), F32)
    lax.fori_loop(0, DN_NCH, scan_body, (s0, s0))

    for y_ref, z_ref, r0, n in ((yc_ref, zc_ref, 0, CTX_LEN), (yl_ref, zl_ref, CTX_LEN, SEQ)):
        o = o_ref[0, r0:r0 + n, :] + o_ref[1, r0:r0 + n, :]
        z = z_ref[...]
        y = o * lax.rsqrt(jnp.mean(o * o, axis=-1, keepdims=True) + EPS) * nw_ref[...]
        y_ref[...] = y * (z * jax.nn.sigmoid(z))


def _deltanet(p_all, conv_w, a_log, dt_bias, norm_w, nb):
    ctx_blk0 = nb * SEQ // CTX_LEN
    qcol = OFF_DN // LANE
    zcol = OFF_DZ // LANE
    abcol = OFF_DA // LANE
    lat = lambda col: pl.BlockSpec((SEQ, LANE), lambda b, h: (b, col + h))
    ctx = lambda col: pl.BlockSpec((CTX_LEN, LANE), lambda b, h: (ctx_blk0 + b, col + h))
    cw = lambda k: pl.BlockSpec((3, LANE), lambda b, h: (0, k * DN_HEADS + h))
    small = pl.BlockSpec((1, LANE), lambda b, h: (0, 0))
    pad = lambda v: jnp.pad(v.reshape(1, -1), ((0, 0), (0, LANE - v.size)))
    n_idx = 2 * DN_NCH
    return pl.pallas_call(
        _deltanet_kernel,
        grid=(nb, DN_HEADS),
        in_specs=[ctx(qcol), ctx(qcol + DN_HEADS), ctx(qcol + 2 * DN_HEADS), ctx(zcol),
                  pl.BlockSpec((CTX_LEN, LANE), lambda b, h: (ctx_blk0 + b, abcol)),
                  lat(qcol), lat(qcol + DN_HEADS), lat(qcol + 2 * DN_HEADS), lat(zcol),
                  pl.BlockSpec((SEQ, LANE), lambda b, h: (b, abcol)),
                  cw(0), cw(1), cw(2), small, small, small],
        out_specs=[pl.BlockSpec((SEQ, LANE), lambda b, h: (b, h)),
                   pl.BlockSpec((CTX_LEN, LANE), lambda b, h: (b, h))],
        out_shape=[jax.ShapeDtypeStruct((nb * SEQ, BR_W), F32),
                   jax.ShapeDtypeStruct((nb * CTX_LEN, BR_W), F32)],
        scratch_shapes=[
            pltpu.VMEM((DN_LAT0 + SEQ + HALO, LANE), F32),
            pltpu.VMEM((DN_ROWS, LANE), F32),
            pltpu.VMEM((DN_ROWS, LANE), F32),
            pltpu.VMEM((DN_ROWS, LANE), F32),
            pltpu.VMEM((2, DN_ROWS, LANE), F32),
            pltpu.VMEM((4, DN_ROWS, LANE), F32),
            pltpu.VMEM((2, DN_ROWS, LANE), F32),
            pltpu.VMEM((n_idx, 2 * DN_BLK, LANE), BF16),
            pltpu.VMEM((n_idx, DN_BLK + DN_DK, DN_BLK), BF16),
            pltpu.VMEM((n_idx, HALO, LANE), F32),
            pltpu.VMEM((2, DN_ROWS, LANE), F32),
        ],
        compiler_params=_params(2),
        name="deltanet",
    )(*([p_all] * 10), conv_w, conv_w, conv_w, pad(a_log), pad(dt_bias), norm_w.reshape(1, LANE))


def kernel(x, c, ctx, c_ctx, w_ada, b_ada, norm1, norm2, w_in, attn_qk_gain, attn_lambda, attn_subln,
           hy_conv, hy_w1, hy_b1, hy_freq, hy_w2, hy_b2, hy_w3, hy_bias, dn_conv, dn_a_log,
           dn_dt_bias, dn_norm, w_gate, w_branch, w_o, w_mlp1, w_mlp2):
    x_all = jnp.concatenate([x.reshape(N_LAT, D_MODEL), ctx.reshape(N_CTX, D_MODEL)], axis=0)
    cond = jax.nn.silu(jnp.concatenate([c, c_ctx[None, :]], axis=0))
    rope_tabs = _rope_tables()
    dft_lat = _hyena_fold_tables(SEQ)
    dft_ctx = _hyena_fold_tables(CTX_LEN)
    fn_lat = _fnet_seq_tables(SEQ)
    fn_ctx = _fnet_seq_tables(CTX_LEN)
    fn_group = _fnet_group_tables()

    for li in range(DEPTH):
        need_ctx = li < DEPTH - 1
        lambda_init = 0.8 - 0.6 * math.exp(-0.3 * li)
        mod = (cond @ w_ada[li] + b_ada[li]).reshape(BATCH + 1, 6, D_MODEL)
        mod = jnp.pad(mod, ((0, 0), (0, MOD_ROWS - 6), (0, 0)))
        w_in_pad = jnp.pad(w_in[li], ((0, 0), (0, N_IN_PAD - N_IN))).astype(BF16)

        p_all = _proj_in(x_all, mod, norm1[li][None, :], w_in_pad)

        q_gain = jnp.tile(attn_qk_gain[li, 0], 2)[None, :]
        k_gain = jnp.tile(attn_qk_gain[li, 1], 2)[None, :]
        subln = attn_subln[li][None, :]
        ya = _attention(p_all, attn_lambda[li], q_gain, k_gain, subln, rope_tabs, lambda_init, True, BATCH)

        hy_filt = (hy_w1[li], hy_b1[li], hy_freq[li], hy_w2[li], hy_b2[li], hy_w3[li])
        yb = _hyena_fold(p_all, hy_conv[li], _hyena_filter_spectra(SEQ, *hy_filt), hy_bias[li], dft_lat,
                         SEQ, 0, BATCH)
        yf = _fnet(p_all, fn_group, fn_lat, SEQ, 0, BATCH)
        yd, yd_c = _deltanet(p_all, dn_conv[li], dn_a_log[li], dn_dt_bias[li], dn_norm[li], BATCH)
        merge_w = (w_gate[li].astype(BF16), w_branch[li].astype(BF16), w_o[li].astype(BF16))
        x_all = _merge(x_all, mod, norm1[li][None, :], [ya, yb, yf, yd], *merge_w, 0, N_LAT)
        n_rows = N_LAT
        if need_ctx:
            ya_c = _attention(p_all, attn_lambda[li], q_gain, k_gain, subln, None, lambda_init, False,
                              BATCH)
            yb_c = _hyena_fold(p_all, hy_conv[li], _hyena_filter_spectra(CTX_LEN, *hy_filt), hy_bias[li],
                               dft_ctx, CTX_LEN, N_LAT // CTX_LEN, BATCH)
            yf_c = _fnet(p_all, fn_group, fn_ctx, CTX_LEN, N_LAT // CTX_LEN, BATCH)
            x_all = _merge(x_all, mod, norm1[li][None, :], [ya_c, yb_c, yf_c, yd_c],
                           *merge_w, N_LAT // ROW_TILE, N_CTX)
            n_rows = N_ALL
        x_all = _mlp(x_all, mod, norm2[li][None, :], w_mlp1[li].astype(BF16),
                     w_mlp2[li].astype(BF16), n_rows)
    return x_all[:N_LAT].reshape(BATCH, SEQ, D_MODEL)
```

```python
import functools
import math

import jax
import jax.numpy as jnp
from jax import lax
from jax.experimental import pallas as pl
from jax.experimental.pallas import tpu as pltpu

F32 = jnp.float32
BF16 = jnp.bfloat16

D_MODEL = 1024
BATCH = 16
SEQ = 2048
DEPTH = 4
GRID_W = 64
CTX_LEN = 256
N_BRANCH = 4
BR_W = 512
A_HEADS = 4
A_DQK = 64
A_DV = 2 * A_DQK
ROPE_BASE = 10000.0
HY_W = BR_W
HY_ORDER = 2
HY_SHORT = 3
HY_BANDS = 16
HY_HID = 64
HY_MIN_DECAY = math.log(1e-2) / 1.5
HY_MAX_DECAY = math.log(1e-2) / 0.3
FN_W = BR_W
FN_GROUPS = 4
DN_HEADS = 4
DN_DK = 128
DN_DV = 128
DN_CHUNK = 64
D_FF = 4 * D_MODEL
EPS = 1e-6

OFF_AQ = 0
OFF_AK = OFF_AQ + A_HEADS * 2 * A_DQK
OFF_AV = OFF_AK + A_HEADS * 2 * A_DQK
OFF_HY = OFF_AV + A_HEADS * A_DV
OFF_FN = OFF_HY + (HY_ORDER + 1) * HY_W
OFF_DN = OFF_FN + FN_W
OFF_DZ = OFF_DN + DN_HEADS * (2 * DN_DK + DN_DV)
OFF_DA = OFF_DZ + DN_HEADS * DN_DV
OFF_DB = OFF_DA + 2 * DN_HEADS
N_IN = OFF_DB + 2 * DN_HEADS

LANE = 128
N_IN_PAD = 45 * LANE
N_LAT = BATCH * SEQ
N_CTX = BATCH * CTX_LEN
N_ALL = N_LAT + N_CTX
MOD_ROWS = 8
VMEM_LIMIT = 56 * 1024 * 1024

ROW_TILE = 512
ATT_QT = 1024
ATT_ROWS = 256


def _params(n_axes):
    return pltpu.CompilerParams(dimension_semantics=("arbitrary",) * n_axes,
                                vmem_limit_bytes=VMEM_LIMIT)


def _mod_row(i):
    return jnp.minimum(i // (SEQ // ROW_TILE), BATCH)


def _ada_norm(x, nw, shift, scale):
    ms = jnp.mean(x * x, axis=-1, keepdims=True)
    return (x * lax.rsqrt(ms + EPS) * nw) * (1.0 + scale) + shift


def _proj_in_kernel(x_ref, mod_ref, nw_ref, w_ref, p_ref):
    h = _ada_norm(x_ref[...], nw_ref[...], mod_ref[0, 0:1, :], mod_ref[0, 1:2, :])
    p_ref[...] = jnp.dot(h.astype(BF16), w_ref[...], preferred_element_type=F32)


def _proj_in(x_all, mod, nw, w_in_pad):
    n_tiles = N_ALL // ROW_TILE
    return pl.pallas_call(
        _proj_in_kernel,
        grid=(n_tiles,),
        in_specs=[
            pl.BlockSpec((ROW_TILE, D_MODEL), lambda i: (i, 0)),
            pl.BlockSpec((1, MOD_ROWS, D_MODEL), lambda i: (_mod_row(i), 0, 0)),
            pl.BlockSpec((1, D_MODEL), lambda i: (0, 0)),
            _resident((D_MODEL, N_IN_PAD), lambda i: (0, 0)),
        ],
        out_specs=pl.BlockSpec((ROW_TILE, N_IN_PAD), lambda i: (i, 0)),
        out_shape=jax.ShapeDtypeStruct((N_ALL, N_IN_PAD), F32),
        compiler_params=_params(1),
        name="proj_in",
    )(x_all, mod, nw, w_in_pad)


def _merge_kernel(x_ref, mod_ref, nw_ref, ya_ref, yb_ref, yf_ref, yd_ref, wg_ref, wb_ref, wo_ref,
                  o_ref):
    x = x_ref[...]
    h = _ada_norm(x, nw_ref[...], mod_ref[0, 0:1, :], mod_ref[0, 1:2, :]).astype(BF16)
    acc = None
    for n, y_ref in enumerate((ya_ref, yb_ref, yf_ref, yd_ref)):
        gate = jax.nn.sigmoid(jnp.dot(h, wg_ref[n], preferred_element_type=F32))
        br = jnp.dot(y_ref[...].astype(BF16), wb_ref[n], preferred_element_type=F32)
        acc = gate * br if acc is None else acc + gate * br
    out = jnp.dot(acc.astype(BF16), wo_ref[...], preferred_element_type=F32)
    o_ref[...] = x + mod_ref[0, 2:3, :] * out


def _merge(x_all, mod, nw, ys, wg, wb, wo, row_blk0, n_rows):
    xrow = lambda i: (row_blk0 + i, 0)
    yrow = lambda i: (i, 0)
    const2 = lambda i: (0, 0)
    const3 = lambda i: (0, 0, 0)
    return pl.pallas_call(
        _merge_kernel,
        grid=(n_rows // ROW_TILE,),
        in_specs=[
            pl.BlockSpec((ROW_TILE, D_MODEL), xrow),
            pl.BlockSpec((1, MOD_ROWS, D_MODEL), lambda i: (_mod_row(row_blk0 + i), 0, 0)),
            pl.BlockSpec((1, D_MODEL), const2),
        ] + [pl.BlockSpec((ROW_TILE, BR_W), yrow)] * N_BRANCH + [
            _resident((N_BRANCH, D_MODEL, D_MODEL), const3),
            _resident((N_BRANCH, BR_W, D_MODEL), const3),
            _resident((D_MODEL, D_MODEL), const2),
        ],
        out_specs=pl.BlockSpec((ROW_TILE, D_MODEL), xrow),
        out_shape=jax.ShapeDtypeStruct((N_ALL, D_MODEL), F32),
        input_output_aliases={0: 0},
        compiler_params=_params(1),
        name="merge",
    )(x_all, mod, nw, *ys, wg, wb, wo)


def _mlp_kernel(x_ref, mod_ref, nw_ref, w1_ref, w2_ref, o_ref):
    x = x_ref[...]
    h = _ada_norm(x, nw_ref[...], mod_ref[0, 3:4, :], mod_ref[0, 4:5, :]).astype(BF16)
    u = jnp.maximum(jnp.dot(h, w1_ref[...], preferred_element_type=F32), 0.0)
    out = jnp.dot((u * u).astype(BF16), w2_ref[...], preferred_element_type=F32)
    o_ref[...] = x + mod_ref[0, 5:6, :] * out


def _mlp(x_all, mod, nw, w1, w2, n_rows):
    row = lambda i: (i, 0)
    const2 = lambda i: (0, 0)
    return pl.pallas_call(
        _mlp_kernel,
        grid=(n_rows // ROW_TILE,),
        in_specs=[
            pl.BlockSpec((ROW_TILE, D_MODEL), row),
            pl.BlockSpec((1, MOD_ROWS, D_MODEL), lambda i: (_mod_row(i), 0, 0)),
            pl.BlockSpec((1, D_MODEL), const2),
            _resident((D_MODEL, D_FF), const2),
            _resident((D_FF, D_MODEL), const2),
        ],
        out_specs=pl.BlockSpec((ROW_TILE, D_MODEL), row),
        out_shape=jax.ShapeDtypeStruct((N_ALL, D_MODEL), F32),
        input_output_aliases={0: 0},
        compiler_params=_params(1),
        name="mlp",
    )(x_all, mod, nw, w1, w2)


def _qk_norm(x, gain):
    lane = lax.broadcasted_iota(jnp.int32, x.shape, 1)
    first = lane < A_DQK
    sq = x * x
    s_all = jnp.sum(sq, axis=-1, keepdims=True)
    s_first = jnp.sum(jnp.where(first, sq, 0.0), axis=-1, keepdims=True)
    ms = jnp.where(first, s_first, s_all - s_first) * (1.0 / A_DQK)
    return x * lax.rsqrt(ms + EPS) * gain


def _rope(x, cos, sin_lo, sin_hi):
    up = pltpu.roll(x, LANE - A_DQK // 4, 1)
    down = pltpu.roll(x, A_DQK // 4, 1)
    return x * cos + up * sin_lo + down * sin_hi


def _attn_kernel(lam_ref, qg_ref, kg_ref, sub_ref, q_ref, *rest, with_latent, lambda_init):
    if with_latent:
        (kc_ref, vc_ref, kl_ref, vl_ref, cq_ref, sq_lo_ref, sq_hi_ref,
         ck_ref, sk_lo_ref, sk_hi_ref, o_ref, kn_ref, vn_ref) = rest
    else:
        kc_ref, vc_ref, o_ref, kn_ref, vn_ref = rest

    @pl.when(pl.program_id(2) == 0)
    def _prep_keys():
        vn_ref[:, LANE:] = jnp.ones((vn_ref.shape[0], LANE), BF16)
        kn_ref[0:CTX_LEN, :] = _qk_norm(kc_ref[...], kg_ref[...]).astype(BF16)
        vn_ref[0:CTX_LEN, 0:LANE] = vc_ref[...].astype(BF16)
        if with_latent:
            kl = _qk_norm(kl_ref[...], kg_ref[...])
            kl = _rope(kl, ck_ref[...], sk_lo_ref[...], sk_hi_ref[...])
            kn_ref[CTX_LEN:, :] = kl.astype(BF16)
            vn_ref[CTX_LEN:, 0:LANE] = vl_ref[...].astype(BF16)

    q = _qk_norm(q_ref[...], qg_ref[...])
    if with_latent:
        q = _rope(q, cq_ref[...], sq_lo_ref[...], sq_hi_ref[...])
    q = q * (A_DQK ** -0.5 * math.log2(math.e))
    lane = lax.broadcasted_iota(jnp.int32, q.shape, 1)
    first = lane < A_DQK
    kn = kn_ref[...]
    vn = vn_ref[...]

    n_grp = max(q.shape[0] // ATT_ROWS, 1)
    rows = q.shape[0] // n_grp
    qs = [jnp.where(first if c == 0 else jnp.logical_not(first), q, 0.0)[g * rows:(g + 1) * rows]
          for g in range(n_grp) for c in range(2)]
    s = [lax.dot_general(x.astype(BF16), kn, (((1,), (1,)), ((), ())), preferred_element_type=F32)
         for x in qs]
    e = [jnp.exp2((x - jnp.max(x, axis=-1, keepdims=True)).astype(BF16)) for x in s]
    ov = [jnp.dot(x, vn, preferred_element_type=F32) for x in e]
    oc = [x[:, 0:LANE] / x[:, LANE:LANE + 1] for x in ov]

    lf = lam_ref[...]
    lam = (jnp.exp(jnp.sum(lf[0:1] * lf[1:2], axis=-1, keepdims=True))
           - jnp.exp(jnp.sum(lf[2:3] * lf[3:4], axis=-1, keepdims=True)) + lambda_init)
    o = jnp.concatenate([oc[2 * g] - lam * oc[2 * g + 1] for g in range(n_grp)], axis=0)
    ms = jnp.mean(o * o, axis=-1, keepdims=True)
    o_ref[...] = (o * lax.rsqrt(ms + EPS) * sub_ref[...]) * (1.0 - lambda_init)


def _attention(p, lam_vec, q_gain, k_gain, subln, rope_tabs, lambda_init, with_latent, nb):
    kcol = OFF_AK // LANE
    vcol = OFF_AV // LANE
    ctx_blk0 = nb * SEQ // CTX_LEN
    small = lambda b, h, i: (0, 0)
    if with_latent:
        qt, nq = ATT_QT, SEQ // ATT_QT
        q_spec = pl.BlockSpec((qt, LANE), lambda b, h, i: (b * nq + i, h))
        n_keys = CTX_LEN + SEQ
    else:
        qt, nq = CTX_LEN, 1
        q_spec = pl.BlockSpec((qt, LANE), lambda b, h, i: (ctx_blk0 + b, h))
        n_keys = CTX_LEN
    in_specs = [
        pl.BlockSpec((4, A_DQK), small),
        pl.BlockSpec((1, LANE), small),
        pl.BlockSpec((1, LANE), small),
        pl.BlockSpec((1, LANE), small),
        q_spec,
        pl.BlockSpec((CTX_LEN, LANE), lambda b, h, i: (ctx_blk0 + b, kcol + h)),
        pl.BlockSpec((CTX_LEN, LANE), lambda b, h, i: (ctx_blk0 + b, vcol + h)),
    ]
    args = [lam_vec, q_gain, k_gain, subln, p, p, p]
    if with_latent:
        in_specs += [
            pl.BlockSpec((SEQ, LANE), lambda b, h, i: (b, kcol + h)),
            pl.BlockSpec((SEQ, LANE), lambda b, h, i: (b, vcol + h)),
        ] + [pl.BlockSpec((qt, LANE), lambda b, h, i: (i, 0))] * 3 + [
            pl.BlockSpec((SEQ, LANE), small)] * 3
        args += [p, p] + list(rope_tabs) + list(rope_tabs)
    n_rows = nb * (SEQ if with_latent else CTX_LEN)
    return pl.pallas_call(
        functools.partial(_attn_kernel, with_latent=with_latent, lambda_init=lambda_init),
        grid=(nb, A_HEADS, nq),
        in_specs=in_specs,
        out_specs=pl.BlockSpec((qt, LANE), lambda b, h, i: (b * nq + i, h)),
        out_shape=jax.ShapeDtypeStruct((n_rows, A_HEADS * A_DV), F32),
        scratch_shapes=[pltpu.VMEM((n_keys, LANE), BF16), pltpu.VMEM((n_keys, 2 * LANE), BF16)],
        compiler_params=_params(3),
        name="diff_attn_latent" if with_latent else "diff_attn_ctx",
    )(*args)


def _rope_tables():
    t = jnp.arange(SEQ, dtype=jnp.int32)
    pos = jnp.stack([(t // GRID_W).astype(F32), (t % GRID_W).astype(F32)], axis=1)
    n_freq = A_DQK // 4
    inv = ROPE_BASE ** (-jnp.arange(n_freq, dtype=F32) / n_freq)
    ang = pos[:, :, None] * inv
    cos = jnp.cos(ang)[:, None, :, None, :]
    sin = jnp.sin(ang)[:, None, :, None, :]
    shape = (SEQ, 2, 2, 2, n_freq)
    half = jnp.arange(2).reshape(1, 1, 1, 2, 1)
    cos_t = jnp.broadcast_to(cos, shape).reshape(SEQ, LANE)
    sin_lo = jnp.where(half == 0, -jnp.broadcast_to(sin, shape), 0.0).reshape(SEQ, LANE)
    sin_hi = jnp.where(half == 1, jnp.broadcast_to(sin, shape), 0.0).reshape(SEQ, LANE)
    return cos_t, sin_lo, sin_hi


def _hyena_filter_spectra(length, w1, b1, freq, w2, b2, w3):
    t = jnp.linspace(0.0, 1.0, length, dtype=F32)[:, None]
    w = 2.0 * math.pi * jnp.arange(length, dtype=F32)[:, None] / length
    bands = jnp.linspace(1e-4, HY_BANDS - 1, HY_BANDS, dtype=F32)[None]
    feats = jnp.concatenate([t, jnp.cos(bands * w), -jnp.sin(bands * w)], axis=-1)
    hdn = jnp.sin(freq[0] * (feats @ w1 + b1))
    hdn = jnp.sin(freq[1] * (hdn @ w2 + b2))
    h = (hdn @ w3).reshape(length, HY_ORDER, 2, HY_W)
    deltas = jnp.abs(jnp.linspace(HY_MIN_DECAY, HY_MAX_DECAY, HY_W, dtype=F32))
    h = h * jnp.exp(-t[:, :, None, None] * deltas)
    past = h[:, :, 0].reshape(length, HY_ORDER * HY_W)
    fut = h[:, :, 1].at[0].set(0.0).reshape(length, HY_ORDER * HY_W)
    norm = jnp.sum(jnp.abs(past) + jnp.abs(fut), axis=0, keepdims=True)
    return _tap_spectrum((past + fut) / norm, (past - fut) / norm, length)


def _tap_spectrum_kernel(sum_ref, dif_ref, c_ref, s_ref, kr_ref, ki_ref, kn_ref):
    a = sum_ref[...]
    kr_ref[...] = jnp.dot(c_ref[...], a.astype(BF16), preferred_element_type=F32)
    ki_ref[...] = -jnp.dot(s_ref[...], dif_ref[...].astype(BF16), preferred_element_type=F32)
    t = lax.broadcasted_iota(jnp.int32, (a.shape[0], 1), 0)
    nyq = jnp.sum(a * (1 - 2 * (t & 1)).astype(F32), axis=0, keepdims=True)
    kn_ref[...] = jnp.broadcast_to(nyq, kn_ref.shape)


def _tap_spectrum(tap_sum, tap_dif, length):
    n = tap_sum.shape[1]
    cos, sin = (t.astype(BF16) for t in _dft_tables(length, 2 * length))
    col = pl.BlockSpec((length, HY_CB), lambda c: (0, c))
    table = _resident((length, length), lambda c: (0, 0))
    kr, ki, kn = pl.pallas_call(
        _tap_spectrum_kernel,
        grid=(n // HY_CB,),
        in_specs=[col, col, table, table],
        out_specs=[col, col, pl.BlockSpec((HALO, HY_CB), lambda c: (0, c))],
        out_shape=[jax.ShapeDtypeStruct((length, n), F32)] * 2 + [jax.ShapeDtypeStruct((HALO, n), F32)],
        compiler_params=_params(1),
        name="tap_spectrum_%d" % length,
    )(tap_sum, tap_dif, cos, sin)
    shape = (length + 1, HY_ORDER, HY_W)
    return (jnp.concatenate([kr, kn[0:1]], axis=0).reshape(shape),
            jnp.concatenate([ki, jnp.zeros((1, n), F32)], axis=0).reshape(shape))


def _dft_tables(n, period):
    idx = jnp.arange(n, dtype=jnp.int32)
    ang = ((idx[:, None] * idx[None, :]) % period).astype(F32) * (2.0 * math.pi / period)
    return jnp.cos(ang), jnp.sin(ang)


def _resident(shape, index_map):
    return pl.BlockSpec(shape, index_map, pipeline_mode=pl.Buffered(1))


HY_CB = 256
HY_ROWS = 512
HALO = 8


def _hyena_fold_kernel(*refs, length, rows):
    tiles = HY_CB // LANE
    x1_refs, x2_refs, v_refs = refs[0:tiles], refs[tiles:2 * tiles], refs[2 * tiles:3 * tiles]
    (cw1_ref, cw2_ref, cwv_ref, kt_ref, km_ref, bias_ref, cfe_ref, sfe_ref, cfo_ref, sfo_ref, cfot_ref,
     sfot_ref, oe_ref, oo_ref, pe_ref, po_ref, ze_ref, zo_ref, zeb_ref, zob_ref, ya_ref, yb_ref, yc_ref,
     yd_ref) = refs[3 * tiles:]
    half = length // 2
    starts = range(0, half, rows)
    zeros = jnp.zeros((HALO, HY_CB), F32)
    for ref in (pe_ref, po_ref):
        ref[0:HALO, :] = zeros
        ref[HALO + half:, :] = zeros

    def load_split(part_refs):
        for t, ref in enumerate(part_refs):
            pe_ref[HALO:HALO + half, t * LANE:(t + 1) * LANE] = ref[pl.ds(0, half, stride=2), :]
            po_ref[HALO:HALO + half, t * LANE:(t + 1) * LANE] = ref[pl.ds(1, half, stride=2), :]

    def short_conv(w_ref, r0):
        w = w_ref[...]
        lo, hi = r0 + HALO, r0 + HALO + rows
        xe, xo = pe_ref[lo:hi, :], po_ref[lo:hi, :]
        even = po_ref[lo - 1:hi - 1, :] * w[0:1] + xe * w[1:2] + xo * w[2:3]
        odd = xe * w[0:1] + xo * w[1:2] + pe_ref[lo + 1:hi + 1, :] * w[2:3]
        return even, odd

    def put_z(r0, even, odd):
        ze_ref[r0:r0 + rows, :] = even
        zo_ref[r0:r0 + rows, :] = odd
        zeb_ref[r0:r0 + rows, :] = even.astype(BF16)
        zob_ref[r0:r0 + rows, :] = odd.astype(BF16)

    j = lax.broadcasted_iota(jnp.int32, (rows, 1), 0)
    alt = (1 - 2 * (j & 1)).astype(F32)

    def mm(t_ref, r0, x):
        return jnp.dot(t_ref[r0:r0 + rows, :], x, preferred_element_type=F32)

    load_split(v_refs)
    for r0 in starts:
        put_z(r0, *short_conv(cwv_ref, r0))

    for n, (gate_refs, gw_ref) in enumerate(((x1_refs, cw1_ref), (x2_refs, cw2_ref))):
        load_split(gate_refs)
        zr_mid = jnp.zeros((1, HY_CB), F32)
        zi_mid = jnp.zeros((1, HY_CB), F32)
        for r0 in starts:
            zeb, zob = zeb_ref[...], zob_ref[...]
            a, b = mm(cfe_ref, r0, zeb), mm(cfo_ref, r0, zob)
            a_s, b_s = mm(sfe_ref, r0, zeb), mm(sfo_ref, r0, zob)
            blk = slice(r0, r0 + rows)
            kr_lo, ki_lo, kr_hi, ki_hi = (kt_ref[n, i, blk, :] for i in range(4))
            zr, zi = a + b, a_s + b_s
            yr_lo, yi_lo = zr * kr_lo + zi * ki_lo, zr * ki_lo - zi * kr_lo
            zr, zi = a - b, b_s - a_s
            yr_hi, yi_hi = zr * kr_hi + zi * ki_hi, zr * ki_hi - zi * kr_hi
            ya_ref[blk, :] = (yr_lo + yr_hi).astype(BF16)
            yb_ref[blk, :] = (yi_lo - yi_hi).astype(BF16)
            yc_ref[blk, :] = (yr_lo - yr_hi).astype(BF16)
            yd_ref[blk, :] = (yi_lo + yi_hi).astype(BF16)
            zr_mid = zr_mid + jnp.sum(ze_ref[blk, :] * alt, axis=0, keepdims=True)
            zi_mid = zi_mid + jnp.sum(zo_ref[blk, :] * alt, axis=0, keepdims=True)
        kr_mid, ki_mid = km_ref[2 * n:2 * n + 1, :], km_ref[2 * n + 1:2 * n + 2, :]
        yr_mid = zr_mid * kr_mid + zi_mid * ki_mid
        yi_mid = zr_mid * ki_mid - zi_mid * kr_mid
        bias = bias_ref[n:n + 1, :]
        for r0 in starts:
            blk = slice(r0, r0 + rows)
            y_even = mm(cfe_ref, r0, ya_ref[...]) - mm(sfe_ref, r0, yb_ref[...]) + alt * yr_mid
            y_odd = mm(cfot_ref, r0, yc_ref[...]) - mm(sfot_ref, r0, yd_ref[...]) - alt * yi_mid
            g_even, g_odd = short_conv(gw_ref, r0)
            put_z(r0, g_even * (y_even + bias * ze_ref[blk, :]), g_odd * (y_odd + bias * zo_ref[blk, :]))

    oe_ref[...] = ze_ref[...]
    oo_ref[...] = zo_ref[...]


def _hyena_fold_tables(length):
    half = length // 2
    f = jnp.arange(half, dtype=jnp.int32)[:, None]
    j = jnp.arange(half, dtype=jnp.int32)[None, :]
    unit = math.pi / length
    even = ((f * 2 * j) % (2 * length)).astype(F32) * unit
    odd = ((f * (2 * j + 1)) % (2 * length)).astype(F32) * unit
    cfo, sfo = jnp.cos(odd), jnp.sin(odd)
    return tuple(t.astype(BF16) for t in (jnp.cos(even), jnp.sin(even), cfo, sfo, cfo.T, sfo.T))


def _hyena_fold(p_all, conv_w, kf, bias, tables, length, row_blk0, nb):
    half = length // 2
    k_re, k_im = kf
    wt = jnp.full((half, 1, 1), 1.0 / length, F32).at[0].set(0.5 / length)
    kt = jnp.stack([k_re[:half] * wt, k_im[:half] * wt,
                    k_re[length:half:-1] * wt, k_im[length:half:-1] * wt], axis=0)
    kt = jnp.transpose(kt, (2, 0, 1, 3))
    km = jnp.stack([k_re[half], k_im[half]], axis=1).reshape(2 * HY_ORDER, HY_W) * (1.0 / length)
    per = HY_W // HY_CB
    tiles = HY_CB // LANE

    def part(k):
        col = (OFF_HY + k * HY_W) // LANE
        return [pl.BlockSpec((length, LANE), lambda c, b, o=col + t: (row_blk0 + b, o + c * tiles))
                for t in range(tiles)]

    cw = lambda k: _resident((HY_SHORT, HY_CB), lambda c, b: (0, k * per + c))
    table = _resident((half, half), lambda c, b: (0, 0))
    out_spec = pl.BlockSpec((half, HY_CB), lambda c, b: (b, c))
    out_shape = jax.ShapeDtypeStruct((nb * half, HY_W), F32)
    y_even, y_odd = pl.pallas_call(
        functools.partial(_hyena_fold_kernel, length=length, rows=min(half, HY_ROWS)),
        grid=(per, nb),
        in_specs=part(0) + part(1) + part(2) + [
            cw(0), cw(1), cw(2),
            _resident((HY_ORDER, 4, half, HY_CB), lambda c, b: (0, 0, 0, c)),
            _resident((2 * HY_ORDER, HY_CB), lambda c, b: (0, c)),
            _resident((HY_ORDER, HY_CB), lambda c, b: (0, c))] + [table] * 6,
        out_specs=[out_spec, out_spec],
        out_shape=[out_shape, out_shape],
        scratch_shapes=[pltpu.VMEM((half + 2 * HALO, HY_CB), F32)] * 2
        + [pltpu.VMEM((half, HY_CB), F32)] * 2 + [pltpu.VMEM((half, HY_CB), BF16)] * 6,
        compiler_params=_params(2),
        name="hyena_%d" % length,
    )(*([p_all] * (3 * tiles)), conv_w, conv_w, conv_w, kt, km, bias, *tables)
    return jnp.stack([y_even, y_odd], axis=1).reshape(nb * length, HY_W)


FN_CB = 256
FN_GW = FN_W // FN_GROUPS


def _fnet_kernel(*refs, scale, half):
    tiles = FN_CB // LANE
    u_refs = refs[0:tiles]
    cg_ref, sg_ref, ce_ref, se_ref, co_ref, so_ref, o_ref = refs[tiles:]

    def real_part(par, c_ref, s_ref):
        u = jnp.concatenate([r[pl.ds(par, half, stride=2), :] for r in u_refs], axis=1).astype(BF16)
        uc = jnp.dot(u, cg_ref[...], preferred_element_type=F32).astype(BF16)
        us = jnp.dot(u, sg_ref[...], preferred_element_type=F32).astype(BF16)
        return (jnp.dot(c_ref[...], uc, preferred_element_type=F32)
                - jnp.dot(s_ref[...], us, preferred_element_type=F32))

    a = real_part(0, ce_ref, se_ref)
    b = real_part(1, co_ref, so_ref)
    o_ref[0:half, :] = (a + b) * scale
    o_ref[half:, :] = (a - b) * scale


def _fnet_seq_tables(length):
    half = length // 2
    f = jnp.arange(half, dtype=jnp.int32)[:, None]
    j = jnp.arange(half, dtype=jnp.int32)[None, :]
    unit = 2.0 * math.pi / length
    even = ((f * 2 * j) % length).astype(F32) * unit
    odd = ((f * (2 * j + 1)) % length).astype(F32) * unit
    return tuple(t.astype(BF16) for t in (jnp.cos(even), jnp.sin(even), jnp.cos(odd), jnp.sin(odd)))


def _fnet(p_all, group_dft, seq_dft, length, row_blk0, nb):
    half = length // 2
    tiles = FN_CB // LANE
    col0 = OFF_FN // LANE
    small = lambda b, c: (0, 0)
    u_specs = [pl.BlockSpec((length, LANE), lambda b, c, t=t: (row_blk0 + b, col0 + c * tiles + t))
               for t in range(tiles)]
    return pl.pallas_call(
        functools.partial(_fnet_kernel, scale=(length * FN_GW) ** -0.5, half=half),
        grid=(nb, FN_W // FN_CB),
        in_specs=u_specs + [_resident((FN_CB, FN_CB), small)] * 2 + [_resident((half, half), small)] * 4,
        out_specs=pl.BlockSpec((length, FN_CB), lambda b, c: (b, c)),
        out_shape=jax.ShapeDtypeStruct((nb * length, FN_W), F32),
        compiler_params=_params(2),
        name="fnet_%d" % length,
    )(*([p_all] * tiles), *group_dft, *seq_dft)


def _fnet_group_tables():
    cg, sg = _dft_tables(FN_GW, FN_GW)
    eye = jnp.eye(FN_CB // FN_GW, dtype=F32)
    return jnp.kron(eye, cg).astype(BF16), jnp.kron(eye, sg).astype(BF16)


DN_ROWS = CTX_LEN + SEQ
DN_BLK = 256
DN_NCH = DN_ROWS // DN_BLK
DN_CTX_CH = CTX_LEN // DN_BLK
DN_UNROLL = 3
DN_LAT0 = CTX_LEN + 2 * HALO
DN_BASE_LOG2 = 3


def _softplus(x):
    return jnp.maximum(x, 0.0) + jnp.log1p(jnp.exp(-jnp.abs(x)))


def _deltanet_kernel(qc_ref, kc_ref, vc_ref, zc_ref, abc_ref, ql_ref, kl_ref, vl_ref, zl_ref, abl_ref,
                     cwq_ref, cwk_ref, cwv_ref, alog_ref, dtb_ref, nw_ref, yl_ref, yc_ref,
                     pad_ref, qn_ref, kn_ref, vn_ref, gall_ref, gate_ref, u_ref, wq_ref, ik_ref, gl_ref,
                     o_ref):
    head = pl.program_id(1)
    cs = DN_BLK

    zeros = jnp.zeros((HALO, LANE), F32)
    pad_ref[0:HALO, :] = zeros
    pad_ref[HALO + CTX_LEN:DN_LAT0, :] = zeros
    pad_ref[DN_LAT0 + SEQ:, :] = zeros

    def conv_silu(c_ref, l_ref, w_ref):
        pad_ref[HALO:HALO + CTX_LEN, :] = c_ref[...]
        pad_ref[DN_LAT0:DN_LAT0 + SEQ, :] = l_ref[...]
        w = w_ref[...]
        parts = []
        for start, n in ((HALO, CTX_LEN), (DN_LAT0, SEQ)):
            y = (pad_ref[start - 1:start - 1 + n, :] * w[0:1] + pad_ref[start:start + n, :] * w[1:2]
                 + pad_ref[start + 1:start + 1 + n, :] * w[2:3])
            parts.append(y * jax.nn.sigmoid(y))
        return parts

    def l2n(x):
        return x * lax.rsqrt(jnp.sum(x * x, axis=-1, keepdims=True) + EPS)

    for dst_ref, (c_ref, l_ref, w_ref), norm, scale in (
            (qn_ref, (qc_ref, ql_ref, cwq_ref), True, DN_DK ** -0.5),
            (kn_ref, (kc_ref, kl_ref, cwk_ref), True, 1.0),
            (vn_ref, (vc_ref, vl_ref, cwv_ref), False, 1.0)):
        yc, yl = conv_silu(c_ref, l_ref, w_ref)
        if norm:
            yc, yl = l2n(yc) * scale, l2n(yl) * scale
        dst_ref[0:CTX_LEN, :] = yc
        dst_ref[CTX_LEN:, :] = yl

    @pl.when(head == 0)
    def _all_head_gates():
        for ab_ref, r0, n in ((abc_ref, 0, CTX_LEN), (abl_ref, CTX_LEN, SEQ)):
            ab = ab_ref[...]
            gall_ref[0, r0:r0 + n, :] = -jnp.exp(alog_ref[...]) * _softplus(ab + dtb_ref[...])
            gall_ref[1, r0:r0 + n, :] = jax.nn.sigmoid(ab)

    lane = lax.broadcasted_iota(jnp.int32, (1, LANE), 1)
    for d in range(2):
        for k, base in enumerate((0, 2 * DN_HEADS)):
            col = jnp.sum(jnp.where(lane == base + d * DN_HEADS + head, gall_ref[k], 0.0), axis=-1,
                          keepdims=True)
            gate_ref[2 * d + k] = jnp.broadcast_to(col, (DN_ROWS, LANE))

    nt = (((1,), (1,)), ((), ()))

    def mm(x, y):
        return jnp.dot(x.astype(BF16), y.astype(BF16), preferred_element_type=F32)

    def block_body(it, carry):
        ii = lax.broadcasted_iota(jnp.int32, (cs, cs), 0)
        jj = lax.broadcasted_iota(jnp.int32, (cs, cs), 1)
        eye = ii == jj
        incl = (ii >= jj, ii <= jj)
        tri = tuple(jnp.where(m, 1.0, 0.0).astype(BF16) for m in incl)
        blocks = [it * DN_UNROLL + j for j in range(DN_UNROLL)]
        rows = [pl.ds(pl.multiple_of(c * cs, cs), cs) for c in blocks]
        q = [qn_ref[r, :] for r in rows]
        k = [kn_ref[r, :] for r in rows]
        v = [vn_ref[r, :] for r in rows]
        k16 = [x.astype(BF16) for x in k]
        kk = [lax.dot_general(x, x, nt, preferred_element_type=F32) for x in k16]
        qk = [lax.dot_general(x.astype(BF16), y, nt, preferred_element_type=F32) for x, y in zip(q, k16)]
        chains = [(j, d) for j in range(DN_UNROLL) for d in range(2)]
        g = [gate_ref[2 * d, rows[j], :] for j, d in chains]
        beta = [gate_ref[2 * d + 1, rows[j], :] for j, d in chains]
        g_hi = [x.astype(BF16) for x in g]
        g_lo = [(x - h.astype(F32)).astype(BF16) for x, h in zip(g, g_hi)]
        gc = [jnp.dot(tri[d], h, preferred_element_type=F32) + jnp.dot(tri[d], l, preferred_element_type=F32)
              for (j, d), h, l in zip(chains, g_hi, g_lo)]
        gc_last = [x[cs - 1:cs, :] if d == 0 else x[0:1, :] for (j, d), x in zip(chains, gc)]
        decay, a_mat = [], []
        for (j, d), x, bt in zip(chains, gc, beta):
            gc2 = jnp.concatenate([x] * (cs // LANE), axis=1)
            gc_row = jnp.sum(jnp.where(eye, gc2, 0.0), axis=0, keepdims=True)
            dec = jnp.where(incl[d], jnp.exp(jnp.where(incl[d], gc2 - gc_row, 0.0)), 0.0)
            decay.append(dec)
            a_mat.append(jnp.where(eye, 0.0, kk[j] * jnp.concatenate([bt] * (cs // LANE), axis=1) * dec))
        same = (ii >> DN_BASE_LOG2) == (jj >> DN_BASE_LOG2)
        t_mat = [jnp.where(eye, 1.0, 0.0) for _ in chains]
        p_mat = [jnp.where(same, -a, 0.0) for a in a_mat]
        for lvl in range(DN_BASE_LOG2 - 1):
            out = [mm(p, jnp.concatenate([t, p], axis=1)) for t, p in zip(t_mat, p_mat)]
            t_mat = [t + o[:, 0:cs] for t, o in zip(t_mat, out)]
            p_mat = [o[:, cs:] for o in out]
        t_mat = [t + mm(p, t) for t, p in zip(t_mat, p_mat)]
        for s in range(DN_BASE_LOG2, cs.bit_length() - 1):
            e_mask = jnp.logical_and((ii >> (s + 1)) == (jj >> (s + 1)), (ii >> s) != (jj >> s))
            et = [mm(jnp.where(e_mask, a, 0.0), t) for a, t in zip(a_mat, t_mat)]
            t_mat = [t - mm(t, e) for t, e in zip(t_mat, et)]
        e_gc = [jnp.exp(x) for x in gc]
        uw = [mm(t, jnp.concatenate([v[j] * bt, k[j] * bt * e], axis=1))
              for (j, d), t, bt, e in zip(chains, t_mat, beta, e_gc)]
        for n, (j, d) in enumerate(chains):
            idx = d * DN_NCH + blocks[j]
            u_ref[d, rows[j], :] = uw[n][:, 0:LANE]
            wq_ref[idx, 0:cs, :] = uw[n][:, LANE:].astype(BF16)
            wq_ref[idx, cs:, :] = (q[j] * e_gc[n]).astype(BF16)
            ik_ref[idx, 0:cs, :] = (qk[j] * decay[n]).astype(BF16)
            ik_ref[idx, cs:, :] = (k[j] * jnp.exp(gc_last[n] - gc[n])).T.astype(BF16)
            gl_ref[idx] = jnp.broadcast_to(jnp.exp(gc_last[n]), (HALO, LANE))
        return carry

    lax.fori_loop(0, DN_NCH // DN_UNROLL, block_body, 0)

    def scan_body(n, states):
        rev_c = jnp.where(n < DN_CTX_CH, DN_CTX_CH - 1 - n, DN_NCH + DN_CTX_CH - 1 - n)
        blk = (n, rev_c)
        idx = [d * DN_NCH + blk[d] for d in range(2)]
        rows = [pl.ds(pl.multiple_of(blk[d] * cs, cs), cs) for d in range(2)]
        ws_qs = [jnp.dot(wq_ref[idx[d]], states[d].astype(BF16), preferred_element_type=F32)
                 for d in range(2)]
        v_new = [u_ref[d, rows[d], :] - ws_qs[d][0:cs] for d in range(2)]
        r = [jnp.dot(ik_ref[idx[d]], v_new[d].astype(BF16), preferred_element_type=F32) for d in range(2)]
        for d in range(2):
            o_ref[d, rows[d], :] = ws_qs[d][cs:] + r[d][0:cs]
        return tuple(states[d] * gl_ref[idx[d]][0:1, :] + r[d][cs:] for d in range(2))

    s0 = jnp.zeros((DN_DK, DN_DV), F32)
    lax.fori_loop(0, DN_NCH, scan_body, (s0, s0))

    for y_ref, z_ref, r0, n in ((yc_ref, zc_ref, 0, CTX_LEN), (yl_ref, zl_ref, CTX_LEN, SEQ)):
        o = o_ref[0, r0:r0 + n, :] + o_ref[1, r0:r0 + n, :]
        z = z_ref[...]
        y = o * lax.rsqrt(jnp.mean(o * o, axis=-1, keepdims=True) + EPS) * nw_ref[...]
        y_ref[...] = y * (z * jax.nn.sigmoid(z))


def _deltanet(p_all, conv_w, a_log, dt_bias, norm_w, nb):
    ctx_blk0 = nb * SEQ // CTX_LEN
    qcol = OFF_DN // LANE
    zcol = OFF_DZ // LANE
    abcol = OFF_DA // LANE
    lat = lambda col: pl.BlockSpec((SEQ, LANE), lambda b, h: (b, col + h))
    ctx = lambda col: pl.BlockSpec((CTX_LEN, LANE), lambda b, h: (ctx_blk0 + b, col + h))
    cw = lambda k: pl.BlockSpec((3, LANE), lambda b, h: (0, k * DN_HEADS + h))
    small = pl.BlockSpec((1, LANE), lambda b, h: (0, 0))
    pad = lambda v: jnp.pad(v.reshape(1, -1), ((0, 0), (0, LANE - v.size)))
    n_idx = 2 * DN_NCH
    return pl.pallas_call(
        _deltanet_kernel,
        grid=(nb, DN_HEADS),
        in_specs=[ctx(qcol), ctx(qcol + DN_HEADS), ctx(qcol + 2 * DN_HEADS), ctx(zcol),
                  pl.BlockSpec((CTX_LEN, LANE), lambda b, h: (ctx_blk0 + b, abcol)),
                  lat(qcol), lat(qcol + DN_HEADS), lat(qcol + 2 * DN_HEADS), lat(zcol),
                  pl.BlockSpec((SEQ, LANE), lambda b, h: (b, abcol)),
                  cw(0), cw(1), cw(2), small, small, small],
        out_specs=[pl.BlockSpec((SEQ, LANE), lambda b, h: (b, h)),
                   pl.BlockSpec((CTX_LEN, LANE), lambda b, h: (b, h))],
        out_shape=[jax.ShapeDtypeStruct((nb * SEQ, BR_W), F32),
                   jax.ShapeDtypeStruct((nb * CTX_LEN, BR_W), F32)],
        scratch_shapes=[
            pltpu.VMEM((DN_LAT0 + SEQ + HALO, LANE), F32),
            pltpu.VMEM((DN_ROWS, LANE), F32),
            pltpu.VMEM((DN_ROWS, LANE), F32),
            pltpu.VMEM((DN_ROWS, LANE), F32),
            pltpu.VMEM((2, DN_ROWS, LANE), F32),
            pltpu.VMEM((4, DN_ROWS, LANE), F32),
            pltpu.VMEM((2, DN_ROWS, LANE), F32),
            pltpu.VMEM((n_idx, 2 * DN_BLK, LANE), BF16),
            pltpu.VMEM((n_idx, DN_BLK + DN_DK, DN_BLK), BF16),
            pltpu.VMEM((n_idx, HALO, LANE), F32),
            pltpu.VMEM((2, DN_ROWS, LANE), F32),
        ],
        compiler_params=_params(2),
        name="deltanet",
    )(*([p_all] * 10), conv_w, conv_w, conv_w, pad(a_log), pad(dt_bias), norm_w.reshape(1, LANE))


def kernel(x, c, ctx, c_ctx, w_ada, b_ada, norm1, norm2, w_in, attn_qk_gain, attn_lambda, attn_subln,
           hy_conv, hy_w1, hy_b1, hy_freq, hy_w2, hy_b2, hy_w3, hy_bias, dn_conv, dn_a_log,
           dn_dt_bias, dn_norm, w_gate, w_branch, w_o, w_mlp1, w_mlp2):
    x_all = jnp.concatenate([x.reshape(N_LAT, D_MODEL), ctx.reshape(N_CTX, D_MODEL)], axis=0)
    cond = jax.nn.silu(jnp.concatenate([c, c_ctx[None, :]], axis=0))
    rope_tabs = _rope_tables()
    dft_lat = _hyena_fold_tables(SEQ)
    dft_ctx = _hyena_fold_tables(CTX_LEN)
    fn_lat = _fnet_seq_tables(SEQ)
    fn_ctx = _fnet_seq_tables(CTX_LEN)
    fn_group = _fnet_group_tables()

    for li in range(DEPTH):
        need_ctx = li < DEPTH - 1
        lambda_init = 0.8 - 0.6 * math.exp(-0.3 * li)
        mod = (cond @ w_ada[li] + b_ada[li]).reshape(BATCH + 1, 6, D_MODEL)
        mod = jnp.pad(mod, ((0, 0), (0, MOD_ROWS - 6), (0, 0)))
        w_in_pad = jnp.pad(w_in[li], ((0, 0), (0, N_IN_PAD - N_IN))).astype(BF16)

        p_all = _proj_in(x_all, mod, norm1[li][None, :], w_in_pad)

        q_gain = jnp.tile(attn_qk_gain[li, 0], 2)[None, :]
        k_gain = jnp.tile(attn_qk_gain[li, 1], 2)[None, :]
        subln = attn_subln[li][None, :]
        ya = _attention(p_all, attn_lambda[li], q_gain, k_gain, subln, rope_tabs, lambda_init, True, BATCH)

        hy_filt = (hy_w1[li], hy_b1[li], hy_freq[li], hy_w2[li], hy_b2[li], hy_w3[li])
        yb = _hyena_fold(p_all, hy_conv[li], _hyena_filter_spectra(SEQ, *hy_filt), hy_bias[li], dft_lat,
                         SEQ, 0, BATCH)
        yf = _fnet(p_all, fn_group, fn_lat, SEQ, 0, BATCH)
        yd, yd_c = _deltanet(p_all, dn_conv[li], dn_a_log[li], dn_dt_bias[li], dn_norm[li], BATCH)
        merge_w = (w_gate[li].astype(BF16), w_branch[li].astype(BF16), w_o[li].astype(BF16))
        x_all = _merge(x_all, mod, norm1[li][None, :], [ya, yb, yf, yd], *merge_w, 0, N_LAT)
        n_rows = N_LAT
        if need_ctx:
            ya_c = _attention(p_all, attn_lambda[li], q_gain, k_gain, subln, None, lambda_init, False,
                              BATCH)
            yb_c = _hyena_fold(p_all, hy_conv[li], _hyena_filter_spectra(CTX_LEN, *hy_filt), hy_bias[li],
                               dft_ctx, CTX_LEN, N_LAT // CTX_LEN, BATCH)
            yf_c = _fnet(p_all, fn_group, fn_ctx, CTX_LEN, N_LAT // CTX_LEN, BATCH)
            x_all = _merge(x_all, mod, norm1[li][None, :], [ya_c, yb_c, yf_c, yd_c],
                           *merge_w, N_LAT // ROW_TILE, N_CTX)
            n_rows = N_ALL
        x_all = _mlp(x_all, mod, norm2[li][None, :], w_mlp1[li].astype(BF16),
                     w_mlp2[li].astype(BF16), n_rows)
    return x_all[:N_LAT].reshape(BATCH, SEQ, D_MODEL)
```
